```python
import math, functools
import jax, jax.numpy as jnp
from jax import lax
import numpy as np

D_MODEL = 1024
BATCH = 8
SEQ = 2048
DEPTH = 4

PLE_DIM = 256
N_EVEN = (DEPTH + 1) // 2
N_ODD = DEPTH // 2
A_WIDTH = D_MODEL // 2
A_GROUPS = 4
A_GROUP_DIM = A_WIDTH // A_GROUPS
A_CHUNK = 128
B_HEADS = 4
B_HEAD_DIM = (D_MODEL // 2) // B_HEADS
B_WIDTH = B_HEADS * B_HEAD_DIM
B_CONV = 4
B_CHUNK = 64
C_HEADS = 8
C_HEAD_DIM = D_MODEL // (2 * C_HEADS)
C_VALUE_DIM = 2 * C_HEAD_DIM
Q_BLOCK = 128
D_FF = 4 * D_MODEL
EVEN_IN = 2 * A_WIDTH + 4 * B_WIDTH + 2 * B_HEADS
NORM_EPS = 1e-6
RES_SCALE = 0.5

kernel_name = 'hybrid_gmlp_gdn_diffattn_trunk'


def rms_norm(x, gain):
    xf = x.astype(jnp.float32)
    y = xf * lax.rsqrt(jnp.mean(xf * xf, axis=-1, keepdims=True) + NORM_EPS)
    return (y * gain.astype(jnp.float32)).astype(x.dtype)


def layer_norm(x, gain):
    xf = x.astype(jnp.float32)
    xc = xf - jnp.mean(xf, axis=-1, keepdims=True)
    y = xc * lax.rsqrt(jnp.mean(xc * xc, axis=-1, keepdims=True) + NORM_EPS)
    return (y * gain.astype(jnp.float32)).astype(x.dtype)


def l2_normalize(x):
    xf = x.astype(jnp.float32)
    return xf * lax.rsqrt(jnp.sum(xf * xf, axis=-1, keepdims=True) + NORM_EPS)


def causal_depthwise_conv(x, w):
    k_len, ch = w.shape
    return lax.conv_general_dilated(
        x, w[:, None, :].astype(x.dtype), window_strides=(1,), padding=[(k_len - 1, 0)],
        dimension_numbers=('NWC', 'WIO', 'NWC'), feature_group_count=ch)


def gmlp_spatial_gate(uv, v_gain, w_s, b_s):
    bsz, seq, _ = uv.shape
    u, v = jnp.split(uv, 2, axis=-1)
    v = v.reshape(bsz, seq // A_CHUNK, A_CHUNK, A_GROUPS, A_GROUP_DIM)
    v = layer_norm(v, v_gain.reshape(A_GROUPS, A_GROUP_DIM))
    causal = jnp.tril(jnp.ones((A_CHUNK, A_CHUNK), dtype=bool))
    w = jnp.where(causal, w_s, 0.0)
    s = jnp.einsum('gts,bnsgc->bntgc', w, v) + b_s.T[None, None, :, :, None]
    return u * s.reshape(bsz, seq, A_WIDTH)


def chunk_gated_delta_rule(q, k, v, g, beta):
    bsz, seq, heads, dk = q.shape
    dv = v.shape[-1]
    c = B_CHUNK
    n = seq // c
    f32 = jnp.float32

    def chunks(t):
        t = t.astype(f32).reshape((bsz, n, c, heads) + t.shape[3:])
        return jnp.moveaxis(t, 3, 1)

    q, k, v, g, beta = chunks(q), chunks(k), chunks(v), chunks(g), chunks(beta)
    gc = jnp.cumsum(g, axis=-1)
    causal = jnp.tril(jnp.ones((c, c), dtype=bool))
    strict = jnp.tril(jnp.ones((c, c), dtype=bool), k=-1)
    diff = gc[..., :, None] - gc[..., None, :]
    decay = jnp.where(causal, jnp.exp(jnp.where(causal, diff, 0.0)), 0.0)
    k_beta = k * beta[..., None]
    v_beta = v * beta[..., None]
    a = jnp.where(strict, jnp.einsum('bhnid,bhnjd->bhnij', k_beta, k) * decay, 0.0)
    t_mat = a + jnp.eye(c, dtype=f32)
    solve = functools.partial(lax.linalg.triangular_solve, left_side=True, lower=True,
                              unit_diagonal=True)
    u = solve(t_mat, v_beta)
    w = solve(t_mat, k_beta * jnp.exp(gc)[..., None])
    qk = jnp.einsum('bhnid,bhnjd->bhnij', q, k) * decay
    q_dec = q * jnp.exp(gc)[..., None]
    k_dec = k * jnp.exp(gc[..., -1:] - gc)[..., None]
    g_last = jnp.exp(gc[..., -1])
    xs = tuple(jnp.moveaxis(t, 2, 0) for t in (q_dec, k_dec, u, w, qk, g_last))

    def step(state, inp):
        q_c, k_c, u_c, w_c, qk_c, gl_c = inp
        v_new = u_c - jnp.einsum('bhck,bhkv->bhcv', w_c, state)
        o_c = (jnp.einsum('bhck,bhkv->bhcv', q_c, state)
               + jnp.einsum('bhij,bhjv->bhiv', qk_c, v_new))
        state = state * gl_c[..., None, None] + jnp.einsum('bhck,bhcv->bhkv', k_c, v_new)
        return state, o_c

    state0 = jnp.zeros((bsz, heads, dk, dv), f32)
    _, o = lax.scan(step, state0, xs)
    return jnp.transpose(o, (1, 0, 3, 2, 4)).reshape(bsz, seq, heads, dv)


def gated_deltanet(qkv, z, beta_logit, a_logit, conv_w, a_log, dt_bias, out_gain):
    bsz, seq, _ = qkv.shape
    qkv = jax.nn.silu(causal_depthwise_conv(qkv, conv_w))
    q, k, v = jnp.split(qkv, 3, axis=-1)
    shp = (bsz, seq, B_HEADS, B_HEAD_DIM)
    q = l2_normalize(q.reshape(shp)) * (B_HEAD_DIM ** -0.5)
    k = l2_normalize(k.reshape(shp))
    v = v.reshape(shp)
    beta = jax.nn.sigmoid(beta_logit.astype(jnp.float32))
    g = -jnp.exp(a_log.astype(jnp.float32)) * jax.nn.softplus(
        a_logit.astype(jnp.float32) + dt_bias.astype(jnp.float32))
    o = chunk_gated_delta_rule(q, k, v, g, beta).astype(qkv.dtype)
    o = rms_norm(o, out_gain) * jax.nn.silu(z.reshape(shp))
    return o.reshape(bsz, seq, B_WIDTH)


def alibi_slopes(heads):
    return 2.0 ** (-8.0 * jnp.arange(1, heads + 1, dtype=jnp.float32) / heads)


def diff_attention(qkv, q_gain, k_gain, lam_params, sub_gain, lam_init):
    bsz, seq, _ = qkv.shape
    q, k, v = jnp.split(qkv, 3, axis=-1)
    q = rms_norm(q.reshape(bsz, seq, C_HEADS, 2, C_HEAD_DIM), q_gain) * (C_HEAD_DIM ** -0.5)
    k = rms_norm(k.reshape(bsz, seq, C_HEADS, 2, C_HEAD_DIM), k_gain)
    v = v.reshape(bsz, seq, C_HEADS, C_VALUE_DIM)
    lp = lam_params.astype(jnp.float32)
    lam = jnp.exp(jnp.sum(lp[0] * lp[1])) - jnp.exp(jnp.sum(lp[2] * lp[3])) + lam_init
    slopes = alibi_slopes(C_HEADS)[None, :, None, None, None]
    nblk = seq // Q_BLOCK
    qb = jnp.moveaxis(q.reshape(bsz, nblk, Q_BLOCK, C_HEADS, 2, C_HEAD_DIM), 1, 0)
    kpos = jnp.arange(seq)

    def attend(args):
        q_blk, blk = args
        qpos = blk * Q_BLOCK + jnp.arange(Q_BLOCK)
        dist = (qpos[:, None] - kpos[None, :]).astype(jnp.float32)
        logits = jnp.einsum('bqhrd,bkhrd->bhrqk', q_blk, k).astype(jnp.float32) - slopes * dist
        logits = jnp.where(dist >= 0, logits, -jnp.inf)
        probs = jax.nn.softmax(logits, axis=-1)
        attn = probs[:, :, 0] - lam * probs[:, :, 1]
        return jnp.einsum('bhqk,bkhe->bqhe', attn.astype(v.dtype), v)

    o = lax.map(attend, (qb, jnp.arange(nblk)))
    o = jnp.moveaxis(o, 0, 1).reshape(bsz, seq, C_HEADS, C_VALUE_DIM)
    o = rms_norm(o, sub_gain) * (1.0 - lam_init)
    return o.reshape(bsz, seq, C_HEADS * C_VALUE_DIM)


def setup_inputs(seed: int = 0) -> dict:
    key = jax.random.key(seed)
    ks = jax.random.split(key, 32)
    f32 = jnp.float32

    def normal(k, shape, scale):
        return jax.random.normal(k, shape, f32) * scale

    def gain(k, shape):
        return 1.0 + 0.1 * jax.random.normal(k, shape, f32)

    dt = jnp.exp(jax.random.uniform(ks[9], (N_EVEN, B_HEADS), f32,
                                    minval=math.log(1e-3), maxval=math.log(1e-1)))
    return {
        'x': normal(ks[0], (BATCH, SEQ, D_MODEL), 1.0),
        'p': normal(ks[1], (DEPTH, BATCH, SEQ, PLE_DIM), 1.0),
        'ln_mix_e': gain(ks[2], (N_EVEN, D_MODEL)),
        'w_in_e': normal(ks[3], (N_EVEN, D_MODEL, EVEN_IN), D_MODEL ** -0.5),
        'gmlp_v_gain': gain(ks[4], (N_EVEN, A_WIDTH)),
        'gmlp_ws': normal(ks[5], (N_EVEN, A_GROUPS, A_CHUNK, A_CHUNK), A_CHUNK ** -0.5),
        'gmlp_bs': gain(ks[6], (N_EVEN, A_GROUPS, A_CHUNK)),
        'gdn_conv': normal(ks[7], (N_EVEN, B_CONV, 3 * B_WIDTH), B_CONV ** -0.5),
        'gdn_a_log': jnp.log(jax.random.uniform(ks[8], (N_EVEN, B_HEADS), f32, minval=1.0, maxval=16.0)),
        'gdn_dt_bias': dt + jnp.log(-jnp.expm1(-dt)),
        'gdn_out_gain': gain(ks[10], (N_EVEN, B_HEAD_DIM)),
        'w_out_e': normal(ks[11], (N_EVEN, A_WIDTH + B_WIDTH, D_MODEL), (A_WIDTH + B_WIDTH) ** -0.5 * RES_SCALE),
        'ln_mix_o': gain(ks[12], (N_ODD, D_MODEL)),
        'w_qkv_o': normal(ks[13], (N_ODD, D_MODEL, 3 * C_HEADS * C_VALUE_DIM), D_MODEL ** -0.5),
        'attn_q_gain': gain(ks[14], (N_ODD, C_HEAD_DIM)),
        'attn_k_gain': gain(ks[15], (N_ODD, C_HEAD_DIM)),
        'diff_lambda': normal(ks[16], (N_ODD, 4, C_HEAD_DIM), 0.1),
        'attn_sub_gain': gain(ks[17], (N_ODD, C_VALUE_DIM)),
        'w_out_o': normal(ks[18], (N_ODD, C_HEADS * C_VALUE_DIM, D_MODEL), (C_HEADS * C_VALUE_DIM) ** -0.5 * RES_SCALE),
        'ln_mlp': gain(ks[19], (DEPTH, D_MODEL)),
        'w_mlp1': normal(ks[20], (DEPTH, D_MODEL, D_FF), D_MODEL ** -0.5),
        'w_mlp2': normal(ks[21], (DEPTH, D_FF, D_MODEL), D_FF ** -0.5 * RES_SCALE),
        'ln_ple': gain(ks[22], (DEPTH, D_MODEL)),
        'w_ple_gate': normal(ks[23], (DEPTH, D_MODEL, D_MODEL), D_MODEL ** -0.5),
        'w_ple_proj': normal(ks[24], (DEPTH, PLE_DIM, D_MODEL), PLE_DIM ** -0.5 * RES_SCALE),
    }


def reference(x, p, ln_mix_e, w_in_e, gmlp_v_gain, gmlp_ws, gmlp_bs, gdn_conv, gdn_a_log,
              gdn_dt_bias, gdn_out_gain, w_out_e, ln_mix_o, w_qkv_o, attn_q_gain, attn_k_gain,
              diff_lambda, attn_sub_gain, w_out_o, ln_mlp, w_mlp1, w_mlp2, ln_ple, w_ple_gate,
              w_ple_proj):
    split_at = [2 * A_WIDTH, 2 * A_WIDTH + 3 * B_WIDTH, 2 * A_WIDTH + 4 * B_WIDTH,
                2 * A_WIDTH + 4 * B_WIDTH + B_HEADS]
    h = x
    for layer in range(DEPTH):
        i = layer // 2
        if layer % 2 == 0:
            hn = rms_norm(h, ln_mix_e[i])
            proj = hn @ w_in_e[i]
            a_uv, b_qkv, b_z, b_beta, b_a = jnp.split(proj, split_at, axis=-1)
            a_out = gmlp_spatial_gate(jax.nn.gelu(a_uv), gmlp_v_gain[i], gmlp_ws[i], gmlp_bs[i])
            b_out = gated_deltanet(b_qkv, b_z, b_beta, b_a, gdn_conv[i], gdn_a_log[i],
                                   gdn_dt_bias[i], gdn_out_gain[i])
            h = h + jnp.concatenate([a_out, b_out], axis=-1) @ w_out_e[i]
        else:
            hn = rms_norm(h, ln_mix_o[i])
            lam_init = 0.8 - 0.6 * math.exp(-0.3 * layer)
            o = diff_attention(hn @ w_qkv_o[i], attn_q_gain[i], attn_k_gain[i], diff_lambda[i],
                               attn_sub_gain[i], lam_init)
            h = h + o @ w_out_o[i]
        hn = rms_norm(h, ln_mlp[layer])
        h = h + jnp.square(jax.nn.relu(hn @ w_mlp1[layer])) @ w_mlp2[layer]
        gate = jax.nn.sigmoid(rms_norm(h, ln_ple[layer]) @ w_ple_gate[layer])
        h = h + (p[layer] @ w_ple_proj[layer]) * gate
    return h
```

```python
import functools
import math

import jax
import jax.numpy as jnp
from jax import lax
from jax.experimental import pallas as pl
from jax.experimental.pallas import tpu as pltpu

F32 = jnp.float32
BF16 = jnp.bfloat16
NORM_EPS = 1e-6

LANES = 128
D_MODEL = 1024
PLE_DIM = 256
D_FF = 4 * D_MODEL
A_WIDTH = 512
A_GROUPS = 4
A_CHUNK = 128
B_HEADS = 4
B_HEAD_DIM = 128
B_WIDTH = 512
B_CONV = 4
B_CHUNK = 64
CONV_PAD = 8
C_HEADS = 8
C_HEAD_DIM = 64
C_VALUE_DIM = 128
EVEN_MAIN = 2 * A_WIDTH + 4 * B_WIDTH
EVEN_PAD = EVEN_MAIN + LANES
VMEM_LIMIT = 56 * 1024 * 1024


def _rms(x, gain):
    return x * lax.rsqrt(jnp.mean(x * x, axis=-1, keepdims=True) + NORM_EPS) * gain


def _dot(a, b):
    return jnp.dot(a, b, preferred_element_type=F32)


def _dot_nt(a, b):
    return lax.dot_general(a, b, (((1,), (1,)), ((), ())), preferred_element_type=F32)


def _resident(shape):
    zeros = (0,) * len(shape)
    return pl.BlockSpec(shape, lambda *_: zeros, pipeline_mode=pl.Buffered(1))


def _params(*sem):
    return pltpu.CompilerParams(dimension_semantics=sem, vmem_limit_bytes=VMEM_LIMIT)


def _in_even_kernel(h_ref, g_ref, w_ref, o_ref, *, tn):
    hn = _rms(h_ref[...], g_ref[...]).astype(BF16)
    for start in range(0, EVEN_PAD, tn):
        width = min(tn, EVEN_PAD - start)
        y = _dot(hn, w_ref[:, start:start + width])
        if start < 2 * A_WIDTH:
            y = jax.nn.gelu(y)
        o_ref[:, start:start + width] = y


def _in_even(h, gain, w, *, tm=512, tn=512):
    t = h.shape[0]
    return pl.pallas_call(
        functools.partial(_in_even_kernel, tn=tn),
        out_shape=jax.ShapeDtypeStruct((t, EVEN_PAD), F32),
        grid=(t // tm,),
        in_specs=[pl.BlockSpec((tm, D_MODEL), lambda i: (i, 0)),
                  _resident((1, D_MODEL)), _resident((D_MODEL, EVEN_PAD))],
        out_specs=pl.BlockSpec((tm, EVEN_PAD), lambda i: (i, 0)),
        compiler_params=_params("parallel"),
        name="in_even",
    )(h, gain, w)


def _gmlp_kernel(u_ref, v_ref, gain_ref, ws_ref, bs_ref, o_ref, *, tm):
    ii = lax.broadcasted_iota(jnp.int32, (A_CHUNK, A_CHUNK), 0)
    jj = lax.broadcasted_iota(jnp.int32, (A_CHUNK, A_CHUNK), 1)
    for g in range(A_GROUPS):
        cols = slice(g * LANES, (g + 1) * LANES)
        w = jnp.where(ii >= jj, ws_ref[g], 0.0).astype(BF16)
        bias = bs_ref[:, g:g + 1]
        gain = gain_ref[:, cols]
        for c in range(tm // A_CHUNK):
            rows = slice(c * A_CHUNK, (c + 1) * A_CHUNK)
            v = v_ref[rows, cols]
            vc = v - jnp.mean(v, axis=-1, keepdims=True)
            y = vc * lax.rsqrt(jnp.mean(vc * vc, axis=-1, keepdims=True) + NORM_EPS) * gain
            s = _dot(w, y.astype(BF16)) + bias
            o_ref[rows, cols] = (u_ref[rows, cols] * s).astype(BF16)


def _gmlp(proj, v_gain, ws, bs_t, *, tm=512):
    t = proj.shape[0]
    return pl.pallas_call(
        functools.partial(_gmlp_kernel, tm=tm),
        out_shape=jax.ShapeDtypeStruct((t, A_WIDTH), BF16),
        grid=(t // tm,),
        in_specs=[pl.BlockSpec((tm, A_WIDTH), lambda i: (i, 0)),
                  pl.BlockSpec((tm, A_WIDTH), lambda i: (i, 1)),
                  _resident((1, A_WIDTH)), _resident((A_GROUPS, A_CHUNK, A_CHUNK)),
                  _resident((A_CHUNK, A_GROUPS))],
        out_specs=pl.BlockSpec((tm, A_WIDTH), lambda i: (i, 0)),
        compiler_params=_params("parallel"),
        name="gmlp",
    )(proj, proj, v_gain, ws, bs_t)


def _unit_lower_inverse(a_strict, eye, level_masks):
    d = eye - jnp.where(level_masks[0], a_strict, 0.0)
    for mask in level_masks[1:]:
        m = jnp.where(mask, a_strict, 0.0).astype(BF16)
        md = _dot(m, d.astype(BF16))
        d = d - _dot(d.astype(BF16), md.astype(BF16))
    return d


def _gdn_kernel(q_ref, k_ref, v_ref, z_ref, ba_ref, cwq_ref, cwk_ref, cwv_ref, alog_ref, dtb_ref,
                og_ref, o_ref, buf_ref, qn_ref, kn_ref, vn_ref, state_ref, *, tb):
    c_len = B_CHUNK

    @pl.when(pl.program_id(1) == 0)
    def _():
        buf_ref[:, 0:CONV_PAD, :] = jnp.zeros((3, CONV_PAD, B_WIDTH), F32)
        state_ref[...] = jnp.zeros_like(state_ref)

    for idx, (x_ref, cw_ref, dst_ref) in enumerate(
            ((q_ref, cwq_ref, qn_ref), (k_ref, cwk_ref, kn_ref), (v_ref, cwv_ref, vn_ref))):
        x = x_ref[...]
        buf_ref[idx, CONV_PAD:CONV_PAD + tb, :] = x
        cw = cw_ref[...]
        y = cw[B_CONV - 1:B_CONV, :] * x
        for j in range(B_CONV - 1):
            off = CONV_PAD - (B_CONV - 1) + j
            y = y + cw[j:j + 1, :] * buf_ref[idx, off:off + tb, :]
        buf_ref[idx, 0:CONV_PAD, :] = x[tb - CONV_PAD:tb, :]
        y = y * jax.nn.sigmoid(y)
        if idx == 2:
            dst_ref[...] = y
        else:
            scale = B_HEAD_DIM ** -0.5 if idx == 0 else 1.0
            for h in range(B_HEADS):
                cols = slice(h * LANES, (h + 1) * LANES)
                yh = y[:, cols]
                yh = yh * lax.rsqrt(jnp.sum(yh * yh, axis=-1, keepdims=True) + NORM_EPS)
                dst_ref[:, cols] = yh * scale if idx == 0 else yh

    ii = lax.broadcasted_iota(jnp.int32, (c_len, c_len), 0)
    jj = lax.broadcasted_iota(jnp.int32, (c_len, c_len), 1)
    causal = ii >= jj
    strict = ii > jj
    eye = jnp.where(ii == jj, 1.0, 0.0).astype(F32)
    ltri = jnp.where(causal, 1.0, 0.0).astype(BF16)
    level_masks = []
    s = 1
    while s < c_len:
        level_masks.append(((ii // (2 * s)) == (jj // (2 * s))) & ((ii % (2 * s)) >= s) & ((jj % (2 * s)) < s))
        s *= 2
    neg_rate = -jnp.exp(alog_ref[...])
    dt_bias = dtb_ref[...]
    out_gain = og_ref[...]

    def chunk_body(c, carry):
        r0 = pl.multiple_of(c * c_len, c_len)
        rows = pl.ds(r0, c_len)
        ba = ba_ref[rows, :]
        beta = jax.nn.sigmoid(ba)
        g = neg_rate * jax.nn.softplus(ba + dt_bias)
        g_hi = g.astype(BF16)
        g_lo = (g - g_hi.astype(F32)).astype(BF16)
        gc = _dot(ltri, g_hi) + _dot(ltri, g_lo)
        gc_t = gc.T
        for h in range(B_HEADS):
            cols = slice(h * LANES, (h + 1) * LANES)
            q_h = qn_ref[rows, cols]
            k_h = kn_ref[rows, cols]
            v_h = vn_ref[rows, cols]
            gcol = gc[:, B_HEADS + h:B_HEADS + h + 1]
            grow = gc_t[B_HEADS + h:B_HEADS + h + 1, :]
            glast = gc[c_len - 1:c_len, B_HEADS + h:B_HEADS + h + 1]
            decay = jnp.where(causal, jnp.exp(jnp.where(causal, gcol - grow, 0.0)), 0.0)
            bcol = beta[:, h:h + 1]
            k_beta = k_h * bcol
            v_beta = v_h * bcol
            eg = jnp.exp(gcol)
            kk = _dot_nt(jnp.concatenate([q_h, k_beta], axis=0).astype(BF16), k_h.astype(BF16))
            qk = kk[:c_len] * decay
            a_strict = jnp.where(strict, kk[c_len:] * decay, 0.0)
            t_inv = _unit_lower_inverse(a_strict, eye, level_masks)
            uw = _dot(t_inv.astype(BF16), jnp.concatenate([v_beta, k_beta * eg], axis=1).astype(BF16))
            u = uw[:, :B_HEAD_DIM]
            w = uw[:, B_HEAD_DIM:]
            q_dec = q_h * eg
            k_dec = k_h * jnp.exp(glast - gcol)
            state = state_ref[h]
            ws = _dot(jnp.concatenate([w, q_dec], axis=0).astype(BF16), state.astype(BF16))
            v_new = u - ws[:c_len]
            o = ws[c_len:] + _dot(qk.astype(BF16), v_new.astype(BF16))
            state_ref[h] = state * jnp.exp(glast) + _dot(k_dec.T.astype(BF16), v_new.astype(BF16))
            z = z_ref[rows, cols]
            o_ref[rows, cols] = (_rms(o, out_gain) * (z * jax.nn.sigmoid(z))).astype(BF16)
        return carry

    lax.fori_loop(0, tb // c_len, chunk_body, 0)


def _gdn(proj, conv_w, alog_row, dtb_row, out_gain, *, bsz, seq, tb=512):
    t = proj.shape[0]
    nt = seq // tb

    def col(j):
        return pl.BlockSpec((tb, B_WIDTH), lambda b, i: (b * nt + i, j))

    def conv(j):
        return pl.BlockSpec((B_CONV, B_WIDTH), lambda b, i: (0, j))

    return pl.pallas_call(
        functools.partial(_gdn_kernel, tb=tb),
        out_shape=jax.ShapeDtypeStruct((t, B_WIDTH), BF16),
        grid=(bsz, nt),
        in_specs=[col(2), col(3), col(4), col(5),
                  pl.BlockSpec((tb, LANES), lambda b, i: (b * nt + i, EVEN_MAIN // LANES)),
                  conv(0), conv(1), conv(2),
                  pl.BlockSpec((1, LANES), lambda b, i: (0, 0)),
                  pl.BlockSpec((1, LANES), lambda b, i: (0, 0)),
                  pl.BlockSpec((1, B_HEAD_DIM), lambda b, i: (0, 0))],
        out_specs=pl.BlockSpec((tb, B_WIDTH), lambda b, i: (b * nt + i, 0)),
        scratch_shapes=[pltpu.VMEM((3, CONV_PAD + tb, B_WIDTH), F32),
                        pltpu.VMEM((tb, B_WIDTH), F32), pltpu.VMEM((tb, B_WIDTH), F32),
                        pltpu.VMEM((tb, B_WIDTH), F32),
                        pltpu.VMEM((B_HEADS, B_HEAD_DIM, B_HEAD_DIM), F32)],
        compiler_params=_params("parallel", "arbitrary"),
        name="gdn",
    )(proj, proj, proj, proj, proj, conv_w, conv_w, conv_w, alog_row, dtb_row, out_gain)


def _in_odd_kernel(h_ref, g_ref, w_ref, qg_ref, kg_ref, o_ref, *, tm, tn):
    hn = _rms(h_ref[...], g_ref[...]).astype(BF16)
    lane = lax.broadcasted_iota(jnp.int32, (tm, LANES), 1)
    low = lane < C_HEAD_DIM
    qk_width = 2 * C_HEADS * 2 * C_HEAD_DIM
    for start in range(0, 3 * C_HEADS * C_VALUE_DIM, tn):
        y = _dot(hn, w_ref[:, start:start + tn])
        if start >= qk_width:
            o_ref[:, start:start + tn] = y.astype(BF16)
            continue
        is_q = start < qk_width // 2
        gain = qg_ref[...] if is_q else kg_ref[...]
        for j in range(tn // LANES):
            yb = y[:, j * LANES:(j + 1) * LANES]
            y2 = yb * yb
            lo = jnp.sum(jnp.where(low, y2, 0.0), axis=-1, keepdims=True)
            hi = jnp.sum(jnp.where(low, 0.0, y2), axis=-1, keepdims=True)
            ms = jnp.where(low, lo, hi) * (1.0 / C_HEAD_DIM)
            yn = yb * lax.rsqrt(ms + NORM_EPS) * gain
            if is_q:
                yn = yn * (C_HEAD_DIM ** -0.5)
            o_ref[:, start + j * LANES:start + (j + 1) * LANES] = yn.astype(BF16)


def _in_odd(h, gain, w, qg, kg, *, tm=512, tn=512):
    t = h.shape[0]
    n = w.shape[1]
    return pl.pallas_call(
        functools.partial(_in_odd_kernel, tm=tm, tn=tn),
        out_shape=jax.ShapeDtypeStruct((t, n), BF16),
        grid=(t // tm,),
        in_specs=[pl.BlockSpec((tm, D_MODEL), lambda i: (i, 0)),
                  _resident((1, D_MODEL)), _resident((D_MODEL, n)),
                  _resident((1, LANES)), _resident((1, LANES))],
        out_specs=pl.BlockSpec((tm, n), lambda i: (i, 0)),
        compiler_params=_params("parallel"),
        name="in_odd",
    )(h, gain, w, qg, kg)


def _attn_kernel(slopes_ref, q_ref, k_ref, v_ref, lam_ref, sg_ref, o_ref, m_ref, l_ref, acc_ref, *,
                 tq, lam_init):
    tk = tq
    h = pl.program_id(1)
    qi = pl.program_id(2)
    slope = slopes_ref[h]
    ii = lax.broadcasted_iota(jnp.int32, (tq, tk), 0)
    jj = lax.broadcasted_iota(jnp.int32, (tq, tk), 1)
    rel = (jj - ii).astype(F32) * slope
    causal = ii >= jj
    q = q_ref[...]
    lane = lax.broadcasted_iota(jnp.int32, (tq, LANES), 1)
    zero = jnp.zeros_like(q)
    q_half = (jnp.where(lane < C_HEAD_DIM, q, zero), jnp.where(lane < C_HEAD_DIM, zero, q))
    m_ref[...] = jnp.full(m_ref.shape, -jnp.inf, F32)
    l_ref[...] = jnp.zeros(l_ref.shape, F32)
    acc_ref[...] = jnp.zeros(acc_ref.shape, F32)

    def block(kb, masked):
        k0 = pl.multiple_of(kb * tk, tk)
        k = k_ref[pl.ds(k0, tk), :]
        v = v_ref[pl.ds(k0, tk), :]
        offset = slope * ((kb - qi) * tk).astype(F32)
        for r in range(2):
            s = _dot_nt(q_half[r], k) + rel + offset
            if masked:
                s = jnp.where(causal, s, -jnp.inf)
            m_prev = m_ref[r]
            m_new = jnp.maximum(m_prev, jnp.max(s, axis=-1, keepdims=True))
            alpha = jnp.exp(m_prev - m_new)
            p = jnp.exp(s - m_new)
            l_ref[r] = alpha * l_ref[r] + jnp.sum(p, axis=-1, keepdims=True)
            acc_ref[r] = alpha * acc_ref[r] + _dot(p.astype(BF16), v)
            m_ref[r] = m_new

    def body(kb, carry):
        block(kb, False)
        return carry

    lax.fori_loop(0, qi, body, 0)
    block(qi, True)

    lp = lam_ref[...]
    lam = (jnp.exp(jnp.sum(lp[0:1] * lp[1:2], axis=-1, keepdims=True))
           - jnp.exp(jnp.sum(lp[2:3] * lp[3:4], axis=-1, keepdims=True)) + lam_init)
    o = acc_ref[0] / l_ref[0] - lam * (acc_ref[1] / l_ref[1])
    o_ref[...] = (_rms(o, sg_ref[...]) * (1.0 - lam_init)).astype(BF16)


def _attn(qkv, slopes, lam_params, sub_gain, *, bsz, seq, lam_init, tq=256):
    t = qkv.shape[0]
    nq = seq // tq
    kernel = functools.partial(_attn_kernel, tq=tq, lam_init=lam_init)
    return pl.pallas_call(
        kernel,
        out_shape=jax.ShapeDtypeStruct((t, C_HEADS * C_VALUE_DIM), BF16),
        grid_spec=pltpu.PrefetchScalarGridSpec(
            num_scalar_prefetch=1,
            grid=(bsz, C_HEADS, nq),
            in_specs=[pl.BlockSpec((tq, LANES), lambda b, h, i, s: (b * nq + i, h)),
                      pl.BlockSpec((seq, LANES), lambda b, h, i, s: (b, C_HEADS + h)),
                      pl.BlockSpec((seq, LANES), lambda b, h, i, s: (b, 2 * C_HEADS + h)),
                      pl.BlockSpec((4, C_HEAD_DIM), lambda b, h, i, s: (0, 0)),
                      pl.BlockSpec((1, C_VALUE_DIM), lambda b, h, i, s: (0, 0))],
            out_specs=pl.BlockSpec((tq, LANES), lambda b, h, i, s: (b * nq + i, h)),
            scratch_shapes=[pltpu.VMEM((2, tq, 1), F32), pltpu.VMEM((2, tq, 1), F32),
                            pltpu.VMEM((2, tq, C_VALUE_DIM), F32)]),
        compiler_params=_params("parallel", "parallel", "arbitrary"),
        name="diff_attn",
    )(slopes, qkv, qkv, qkv, lam_params, sub_gain)


def _post_kernel(*refs, n_mix, layer, fc):
    h_ref = refs[0]
    mix_refs = refs[1:1 + n_mix]
    (w_out_ref, g_mlp_ref, w1_ref, w2_ref, g_ple_ref, w_gate_ref, w_proj_ref, p_ref, o_ref) = refs[1 + n_mix:]
    del layer
    mix = mix_refs[0][...] if n_mix == 1 else jnp.concatenate([m[...] for m in mix_refs], axis=1)
    h = h_ref[...] + _dot(mix, w_out_ref[...])
    hn = _rms(h, g_mlp_ref[...]).astype(BF16)
    acc = h
    for c in range(0, D_FF, fc):
        a = jnp.maximum(_dot(hn, w1_ref[:, c:c + fc]), 0.0)
        acc = acc + _dot((a * a).astype(BF16), w2_ref[c:c + fc, :])
    h = acc
    gate = jax.nn.sigmoid(_dot(_rms(h, g_ple_ref[...]).astype(BF16), w_gate_ref[...]))
    o_ref[...] = h + _dot(p_ref[...].astype(BF16), w_proj_ref[...]) * gate


def _post(h, mixes, w_out, g_mlp, w1, w2, g_ple, w_gate, w_proj, p_all, layer, *, tm=512, fc=512):
    t = h.shape[0]
    row = lambda i: (i, 0)
    mix_specs = [pl.BlockSpec((tm, m.shape[1]), row) for m in mixes]
    return pl.pallas_call(
        functools.partial(_post_kernel, n_mix=len(mixes), layer=layer, fc=fc),
        out_shape=jax.ShapeDtypeStruct((t, D_MODEL), F32),
        grid=(t // tm,),
        in_specs=[pl.BlockSpec((tm, D_MODEL), row)] + mix_specs + [
            _resident((D_MODEL, D_MODEL)), _resident((1, D_MODEL)),
            _resident((D_MODEL, D_FF)), _resident((D_FF, D_MODEL)),
            _resident((1, D_MODEL)), _resident((D_MODEL, D_MODEL)), _resident((PLE_DIM, D_MODEL)),
            pl.BlockSpec((None, tm, PLE_DIM), lambda i: (layer, i, 0))],
        out_specs=pl.BlockSpec((tm, D_MODEL), row),
        compiler_params=_params("parallel"),
        name="post",
    )(h, *mixes, w_out, g_mlp, w1, w2, g_ple, w_gate, w_proj, p_all)


def _lane_row(values, offset):
    return jnp.zeros((1, LANES), F32).at[0, offset:offset + values.shape[0]].set(values.astype(F32))


def kernel(x, p, ln_mix_e, w_in_e, gmlp_v_gain, gmlp_ws, gmlp_bs, gdn_conv, gdn_a_log, gdn_dt_bias,
           gdn_out_gain, w_out_e, ln_mix_o, w_qkv_o, attn_q_gain, attn_k_gain, diff_lambda,
           attn_sub_gain, w_out_o, ln_mlp, w_mlp1, w_mlp2, ln_ple, w_ple_gate, w_ple_proj):
    bsz, seq, d = x.shape
    depth = p.shape[0]
    t = bsz * seq
    h = x.reshape(t, d)
    p_all = p.reshape(depth, t, PLE_DIM)
    slopes = 2.0 ** (-8.0 * jnp.arange(1, C_HEADS + 1, dtype=F32) / C_HEADS)
    row = lambda a: a.reshape(1, -1).astype(F32)
    for layer in range(depth):
        i = layer // 2
        if layer % 2 == 0:
            w_in = jnp.pad(w_in_e[i], ((0, 0), (0, EVEN_PAD - w_in_e.shape[2]))).astype(BF16)
            proj = _in_even(h, row(ln_mix_e[i]), w_in)
            a_out = _gmlp(proj, row(gmlp_v_gain[i]), gmlp_ws[i], gmlp_bs[i].T)
            b_out = _gdn(proj, gdn_conv[i], _lane_row(gdn_a_log[i], B_HEADS),
                         _lane_row(gdn_dt_bias[i], B_HEADS), row(gdn_out_gain[i]), bsz=bsz, seq=seq)
            mixes = (a_out, b_out)
            w_out = w_out_e[i]
        else:
            lam_init = 0.8 - 0.6 * math.exp(-0.3 * layer)
            qg = row(jnp.concatenate([attn_q_gain[i], attn_q_gain[i]]))
            kg = row(jnp.concatenate([attn_k_gain[i], attn_k_gain[i]]))
            qkv = _in_odd(h, row(ln_mix_o[i]), w_qkv_o[i].astype(BF16), qg, kg)
            o = _attn(qkv, slopes, diff_lambda[i], row(attn_sub_gain[i]), bsz=bsz, seq=seq,
                      lam_init=lam_init)
            mixes = (o,)
            w_out = w_out_o[i]
        h = _post(h, mixes, w_out.astype(BF16), row(ln_mlp[layer]), w_mlp1[layer].astype(BF16),
                  w_mlp2[layer].astype(BF16), row(ln_ple[layer]), w_ple_gate[layer].astype(BF16),
                  w_ple_proj[layer].astype(BF16), p_all, layer)
    return h.reshape(bsz, seq, d)
```

```python
import functools
import math

import jax
import jax.numpy as jnp
from jax import lax
from jax.experimental import pallas as pl
from jax.experimental.pallas import tpu as pltpu

F32 = jnp.float32
BF16 = jnp.bfloat16
NORM_EPS = 1e-6

LANES = 128
D_MODEL = 1024
PLE_DIM = 256
D_FF = 4 * D_MODEL
A_WIDTH = 512
A_GROUPS = 4
A_CHUNK = 128
B_HEADS = 4
B_HEAD_DIM = 128
B_WIDTH = 512
B_CONV = 4
B_CHUNK = 64
CONV_PAD = 8
C_HEADS = 8
C_HEAD_DIM = 64
C_VALUE_DIM = 128
ATTN_BLOCK = 256
EVEN_MAIN = 2 * A_WIDTH + 4 * B_WIDTH
EVEN_PAD = EVEN_MAIN + LANES
VMEM_LIMIT = 56 * 1024 * 1024


def _rms(x, gain):
    return x * lax.rsqrt(jnp.mean(x * x, axis=-1, keepdims=True) + NORM_EPS) * gain


def _dot(a, b):
    return jnp.dot(a, b, preferred_element_type=F32)


def _dot_nt(a, b):
    return lax.dot_general(a, b, (((1,), (1,)), ((), ())), preferred_element_type=F32)


def _resident(shape):
    zeros = (0,) * len(shape)
    return pl.BlockSpec(shape, lambda *_: zeros, pipeline_mode=pl.Buffered(1))


def _params(*sem):
    return pltpu.CompilerParams(dimension_semantics=sem, vmem_limit_bytes=VMEM_LIMIT)


def _in_even_kernel(h_ref, g_ref, w_ref, o_ref, *, tn):
    hn = _rms(h_ref[...], g_ref[...]).astype(BF16)
    for start in range(0, EVEN_PAD, tn):
        width = min(tn, EVEN_PAD - start)
        y = _dot(hn, w_ref[:, start:start + width])
        if start < 2 * A_WIDTH:
            y = jax.nn.gelu(y)
        o_ref[:, start:start + width] = y


def _in_even(h, gain, w, *, tm=512, tn=512):
    t = h.shape[0]
    return pl.pallas_call(
        functools.partial(_in_even_kernel, tn=tn),
        out_shape=jax.ShapeDtypeStruct((t, EVEN_PAD), F32),
        grid=(t // tm,),
        in_specs=[pl.BlockSpec((tm, D_MODEL), lambda i: (i, 0)),
                  _resident((1, D_MODEL)), _resident((D_MODEL, EVEN_PAD))],
        out_specs=pl.BlockSpec((tm, EVEN_PAD), lambda i: (i, 0)),
        compiler_params=_params("parallel"),
        name="in_even",
    )(h, gain, w)


def _gmlp_kernel(u_ref, v_ref, gain_ref, ws_ref, bs_ref, o_ref, *, tm):
    ii = lax.broadcasted_iota(jnp.int32, (A_CHUNK, A_CHUNK), 0)
    jj = lax.broadcasted_iota(jnp.int32, (A_CHUNK, A_CHUNK), 1)
    for g in range(A_GROUPS):
        cols = slice(g * LANES, (g + 1) * LANES)
        w = jnp.where(ii >= jj, ws_ref[g], 0.0).astype(BF16)
        bias = bs_ref[:, g:g + 1]
        gain = gain_ref[:, cols]
        for c in range(tm // A_CHUNK):
            rows = slice(c * A_CHUNK, (c + 1) * A_CHUNK)
            v = v_ref[rows, cols]
            vc = v - jnp.mean(v, axis=-1, keepdims=True)
            y = vc * lax.rsqrt(jnp.mean(vc * vc, axis=-1, keepdims=True) + NORM_EPS) * gain
            s = _dot(w, y.astype(BF16)) + bias
            o_ref[rows, cols] = (u_ref[rows, cols] * s).astype(BF16)


def _gmlp(proj, v_gain, ws, bs_t, *, tm=512):
    t = proj.shape[0]
    return pl.pallas_call(
        functools.partial(_gmlp_kernel, tm=tm),
        out_shape=jax.ShapeDtypeStruct((t, A_WIDTH), BF16),
        grid=(t // tm,),
        in_specs=[pl.BlockSpec((tm, A_WIDTH), lambda i: (i, 0)),
                  pl.BlockSpec((tm, A_WIDTH), lambda i: (i, 1)),
                  _resident((1, A_WIDTH)), _resident((A_GROUPS, A_CHUNK, A_CHUNK)),
                  _resident((A_CHUNK, A_GROUPS))],
        out_specs=pl.BlockSpec((tm, A_WIDTH), lambda i: (i, 0)),
        compiler_params=_params("parallel"),
        name="gmlp",
    )(proj, proj, v_gain, ws, bs_t)


def _unit_lower_inverses(a_list, eye, level_masks):
    ds = [eye - jnp.where(level_masks[0], a, 0.0) for a in a_list]
    for mask in level_masks[1:]:
        d16 = [d.astype(BF16) for d in ds]
        mds = [_dot(jnp.where(mask, a, 0.0).astype(BF16), d) for a, d in zip(a_list, d16)]
        ds = [d - _dot(d_lo, md.astype(BF16)) for d, d_lo, md in zip(ds, d16, mds)]
    return ds


def _gdn_kernel(q_ref, k_ref, v_ref, z_ref, ba_ref, cwq_ref, cwk_ref, cwv_ref, alog_ref, dtb_ref,
                og_ref, o_ref, buf_ref, qn_ref, kn_ref, vn_ref, state_ref, wq_ref, u_ref, qk_ref,
                kdt_ref, gl_ref, *, tb, group):
    c_len = B_CHUNK

    @pl.when(pl.program_id(1) == 0)
    def _():
        buf_ref[:, 0:CONV_PAD, :] = jnp.zeros((3, CONV_PAD, B_WIDTH), F32)
        state_ref[...] = jnp.zeros_like(state_ref)

    for idx, (x_ref, cw_ref, dst_ref) in enumerate(
            ((q_ref, cwq_ref, qn_ref), (k_ref, cwk_ref, kn_ref), (v_ref, cwv_ref, vn_ref))):
        x = x_ref[...]
        buf_ref[idx, CONV_PAD:CONV_PAD + tb, :] = x
        cw = cw_ref[...]
        y = cw[B_CONV - 1:B_CONV, :] * x
        for j in range(B_CONV - 1):
            off = CONV_PAD - (B_CONV - 1) + j
            y = y + cw[j:j + 1, :] * buf_ref[idx, off:off + tb, :]
        buf_ref[idx, 0:CONV_PAD, :] = x[tb - CONV_PAD:tb, :]
        y = y * jax.nn.sigmoid(y)
        if idx == 2:
            dst_ref[...] = y
        else:
            scale = B_HEAD_DIM ** -0.5 if idx == 0 else 1.0
            for h in range(B_HEADS):
                cols = slice(h * LANES, (h + 1) * LANES)
                yh = y[:, cols]
                yh = yh * lax.rsqrt(jnp.sum(yh * yh, axis=-1, keepdims=True) + NORM_EPS)
                dst_ref[:, cols] = yh * scale if idx == 0 else yh

    ii = lax.broadcasted_iota(jnp.int32, (c_len, c_len), 0)
    jj = lax.broadcasted_iota(jnp.int32, (c_len, c_len), 1)
    causal = ii >= jj
    strict = ii > jj
    eye = jnp.where(ii == jj, 1.0, 0.0).astype(F32)
    ltri = jnp.where(causal, 1.0, 0.0).astype(BF16)
    level_masks = []
    s = 1
    while s < c_len:
        level_masks.append(((ii // (2 * s)) == (jj // (2 * s))) & ((ii % (2 * s)) >= s) & ((jj % (2 * s)) < s))
        s *= 2
    neg_rate = -jnp.exp(alog_ref[...])
    dt_bias = dtb_ref[...]
    out_gain = og_ref[...]

    heads = range(B_HEADS)

    def prepare(gi, carry):
        prob = []
        for cc in range(group):
            c = gi * group + cc
            rows = pl.ds(pl.multiple_of(c * c_len, c_len), c_len)
            ba = ba_ref[rows, :]
            beta = jax.nn.sigmoid(ba)
            g = neg_rate * jax.nn.softplus(ba + dt_bias)
            g_hi = g.astype(BF16)
            g_lo = (g - g_hi.astype(F32)).astype(BF16)
            gc = _dot(ltri, g_hi) + _dot(ltri, g_lo)
            gc_t = gc.T
            for h in heads:
                cols = slice(h * LANES, (h + 1) * LANES)
                q_h = qn_ref[rows, cols]
                k_h = kn_ref[rows, cols]
                gcol = gc[:, B_HEADS + h:B_HEADS + h + 1]
                grow = gc_t[B_HEADS + h:B_HEADS + h + 1, :]
                glast = gc[c_len - 1:c_len, B_HEADS + h:B_HEADS + h + 1]
                bcol = beta[:, h:h + 1]
                eg = jnp.exp(gcol)
                k_beta = k_h * bcol
                prob.append(dict(
                    c=c, h=h, q_dec=q_h * eg,
                    decay=jnp.where(causal, jnp.exp(jnp.where(causal, gcol - grow, 0.0)), 0.0),
                    lhs=jnp.concatenate([q_h, k_beta], axis=0).astype(BF16), k16=k_h.astype(BF16),
                    rhs=jnp.concatenate([vn_ref[rows, cols] * bcol, k_beta * eg], axis=1).astype(BF16),
                    k_dec=k_h * jnp.exp(glast - gcol), gl=jnp.exp(glast)))
        kks = [_dot_nt(p["lhs"], p["k16"]) for p in prob]
        a_list = [jnp.where(strict, kk[c_len:] * p["decay"], 0.0) for kk, p in zip(kks, prob)]
        t_invs = _unit_lower_inverses(a_list, eye, level_masks)
        uws = [_dot(t.astype(BF16), p["rhs"]) for t, p in zip(t_invs, prob)]
        for p, kk, uw in zip(prob, kks, uws):
            c, h = p["c"], p["h"]
            u_ref[c, h] = uw[:, :B_HEAD_DIM]
            wq_ref[c, h] = jnp.concatenate([uw[:, B_HEAD_DIM:], p["q_dec"]], axis=0).astype(BF16)
            qk_ref[c, h] = (kk[:c_len] * p["decay"]).astype(BF16)
            kdt_ref[c, h] = p["k_dec"].T.astype(BF16)
            gl_ref[c, h] = jnp.broadcast_to(p["gl"], (1, LANES))
        return carry

    lax.fori_loop(0, tb // (c_len * group), prepare, 0)

    def recur(c, carry):
        rows = pl.ds(pl.multiple_of(c * c_len, c_len), c_len)
        states = [state_ref[h] for h in heads]
        wss = [_dot(wq_ref[c, h], states[h].astype(BF16)) for h in heads]
        v_new = [(u_ref[c, h] - wss[h][:c_len]).astype(BF16) for h in heads]
        outs = [wss[h][c_len:] + _dot(qk_ref[c, h], v_new[h]) for h in heads]
        for h in heads:
            state_ref[h] = states[h] * gl_ref[c, h] + _dot(kdt_ref[c, h], v_new[h])
        for h in heads:
            cols = slice(h * LANES, (h + 1) * LANES)
            z = z_ref[rows, cols]
            o_ref[rows, cols] = (_rms(outs[h], out_gain) * (z * jax.nn.sigmoid(z))).astype(BF16)
        return carry

    lax.fori_loop(0, tb // c_len, recur, 0)


def _gdn(proj, conv_w, alog_row, dtb_row, out_gain, *, bsz, seq, tb=512, group=2):
    t = proj.shape[0]
    nt = seq // tb

    def col(j):
        return pl.BlockSpec((tb, B_WIDTH), lambda b, i: (b * nt + i, j))

    def conv(j):
        return pl.BlockSpec((B_CONV, B_WIDTH), lambda b, i: (0, j))

    nc = tb // B_CHUNK
    per_problem = lambda *shape_dtype: pltpu.VMEM((nc, B_HEADS) + shape_dtype[:-1], shape_dtype[-1])
    return pl.pallas_call(
        functools.partial(_gdn_kernel, tb=tb, group=group),
        out_shape=jax.ShapeDtypeStruct((t, B_WIDTH), BF16),
        grid=(bsz, nt),
        in_specs=[col(2), col(3), col(4), col(5),
                  pl.BlockSpec((tb, LANES), lambda b, i: (b * nt + i, EVEN_MAIN // LANES)),
                  conv(0), conv(1), conv(2),
                  pl.BlockSpec((1, LANES), lambda b, i: (0, 0)),
                  pl.BlockSpec((1, LANES), lambda b, i: (0, 0)),
                  pl.BlockSpec((1, B_HEAD_DIM), lambda b, i: (0, 0))],
        out_specs=pl.BlockSpec((tb, B_WIDTH), lambda b, i: (b * nt + i, 0)),
        scratch_shapes=[pltpu.VMEM((3, CONV_PAD + tb, B_WIDTH), F32),
                        pltpu.VMEM((tb, B_WIDTH), F32), pltpu.VMEM((tb, B_WIDTH), F32),
                        pltpu.VMEM((tb, B_WIDTH), F32),
                        pltpu.VMEM((B_HEADS, B_HEAD_DIM, B_HEAD_DIM), F32),
                        per_problem(2 * B_CHUNK, B_HEAD_DIM, BF16),
                        per_problem(B_CHUNK, B_HEAD_DIM, F32),
                        per_problem(B_CHUNK, B_CHUNK, BF16),
                        per_problem(B_HEAD_DIM, B_CHUNK, BF16),
                        per_problem(1, LANES, F32)],
        compiler_params=_params("parallel", "arbitrary"),
        name="gdn",
    )(proj, proj, proj, proj, proj, conv_w, conv_w, conv_w, alog_row, dtb_row, out_gain)


def _in_odd_kernel(h_ref, g_ref, w_ref, qg_ref, kg_ref, o_ref, vt_ref, *, tm, tn, tk):
    hn = _rms(h_ref[...], g_ref[...]).astype(BF16)
    lane = lax.broadcasted_iota(jnp.int32, (tm, LANES), 1)
    low = lane < C_HEAD_DIM
    qk_width = 2 * C_HEADS * 2 * C_HEAD_DIM
    for start in range(0, 3 * C_HEADS * C_VALUE_DIM, tn):
        y = _dot(hn, w_ref[:, start:start + tn])
        if start >= qk_width:
            for kb in range(tm // tk):
                vt_ref[kb, start - qk_width:start - qk_width + tn, :] = (
                    y[kb * tk:(kb + 1) * tk, :].T.astype(BF16))
            continue
        is_q = start < qk_width // 2
        gain = qg_ref[...] if is_q else kg_ref[...]
        for j in range(tn // LANES):
            yb = y[:, j * LANES:(j + 1) * LANES]
            y2 = yb * yb
            lo = jnp.sum(jnp.where(low, y2, 0.0), axis=-1, keepdims=True)
            hi = jnp.sum(jnp.where(low, 0.0, y2), axis=-1, keepdims=True)
            ms = jnp.where(low, lo, hi) * (1.0 / C_HEAD_DIM)
            yn = yb * lax.rsqrt(ms + NORM_EPS) * gain
            if is_q:
                yn = yn * (C_HEAD_DIM ** -0.5)
            o_ref[:, start + j * LANES:start + (j + 1) * LANES] = yn.astype(BF16)


def _in_odd(h, gain, w, qg, kg, *, bsz, seq, tk, tm=512, tn=512):
    t = h.shape[0]
    n = w.shape[1]
    nt = seq // tm
    qk_width = 2 * C_HEADS * 2 * C_HEAD_DIM
    return pl.pallas_call(
        functools.partial(_in_odd_kernel, tm=tm, tn=tn, tk=tk),
        out_shape=(jax.ShapeDtypeStruct((t, qk_width), BF16),
                   jax.ShapeDtypeStruct((bsz, seq // tk, n - qk_width, tk), BF16)),
        grid=(t // tm,),
        in_specs=[pl.BlockSpec((tm, D_MODEL), lambda i: (i, 0)),
                  _resident((1, D_MODEL)), _resident((D_MODEL, n)),
                  _resident((1, LANES)), _resident((1, LANES))],
        out_specs=(pl.BlockSpec((tm, qk_width), lambda i: (i, 0)),
                   pl.BlockSpec((None, tm // tk, n - qk_width, tk), lambda i: (i // nt, i % nt, 0, 0))),
        compiler_params=_params("parallel"),
        name="in_odd",
    )(h, gain, w, qg, kg)


def _attn_kernel(slopes_ref, q_ref, k_ref, vt_ref, lam_ref, sg_ref, o_ref, s_ref, rel_ref, m_ref, l_ref,
                 acc_ref, *, tq, tk, lam_init):
    slope = slopes_ref[pl.program_id(1)]
    nq = q_ref.shape[0] // tq
    ki = lax.broadcasted_iota(jnp.int32, (tk, 2 * tq), 0)
    qj = lax.broadcasted_iota(jnp.int32, (tk, 2 * tq), 1)
    qj = jnp.where(qj >= tq, qj - tq, qj)
    rel_ref[...] = (ki - qj).astype(F32) * slope
    lane = lax.broadcasted_iota(jnp.int32, (tq, LANES), 1)
    lp = lam_ref[...]
    lam = (jnp.exp(jnp.sum(lp[0:1] * lp[1:2], axis=-1, keepdims=True))
           - jnp.exp(jnp.sum(lp[2:3] * lp[3:4], axis=-1, keepdims=True)) + lam_init)
    sub_gain = sg_ref[...]

    def q_block(qi, carry):
        q0 = pl.multiple_of(qi * tq, tq)
        q = q_ref[pl.ds(q0, tq), :]
        zero = jnp.zeros_like(q)
        q2 = jnp.concatenate([jnp.where(lane < C_HEAD_DIM, q, zero),
                              jnp.where(lane < C_HEAD_DIM, zero, q)], axis=0)
        m_ref[...] = jnp.full(m_ref.shape, -jnp.inf, F32)
        l_ref[...] = jnp.zeros(l_ref.shape, F32)
        acc_ref[...] = jnp.zeros(acc_ref.shape, F32)
        s_ref[0] = _dot_nt(k_ref[0:tk, :], q2)

        def step(kb, masked):
            par = kb % 2
            s = s_ref[par] + rel_ref[...]
            if masked:
                s = jnp.where(ki <= qj, s, -jnp.inf)
            else:
                k_next = k_ref[pl.ds(pl.multiple_of((kb + 1) * tk, tk), tk), :]
                s_ref[1 - par] = _dot_nt(k_next, q2)
            offset = slope * (kb * tk - q0).astype(F32)
            m_prev = m_ref[...]
            m_new = jnp.maximum(m_prev, jnp.max(s, axis=0, keepdims=True) + offset)
            alpha = jnp.exp(m_prev - m_new)
            p = jnp.exp(s + (offset - m_new))
            l_ref[...] = alpha * l_ref[...] + jnp.sum(p, axis=0, keepdims=True)
            acc_ref[...] = alpha * acc_ref[...] + _dot(vt_ref[kb], p.astype(BF16))
            m_ref[...] = m_new

        def body(kb, c):
            step(kb, False)
            return c

        lax.fori_loop(0, qi, body, 0)
        step(qi, True)
        o2 = acc_ref[...] / l_ref[...]
        o_t = o2[:, :tq] - lam * o2[:, tq:]
        o_ref[pl.ds(q0, tq), :] = (_rms(o_t.T, sub_gain) * (1.0 - lam_init)).astype(BF16)
        return carry

    lax.fori_loop(0, nq, q_block, 0)


def _attn(qk, vt, slopes, lam_params, sub_gain, *, bsz, seq, lam_init, tq):
    t = qk.shape[0]
    tk = vt.shape[3]
    assert tk == tq
    kernel = functools.partial(_attn_kernel, tq=tq, tk=tk, lam_init=lam_init)
    return pl.pallas_call(
        kernel,
        out_shape=jax.ShapeDtypeStruct((t, C_HEADS * C_VALUE_DIM), BF16),
        grid_spec=pltpu.PrefetchScalarGridSpec(
            num_scalar_prefetch=1,
            grid=(bsz, C_HEADS),
            in_specs=[pl.BlockSpec((seq, LANES), lambda b, h, s: (b, h)),
                      pl.BlockSpec((seq, LANES), lambda b, h, s: (b, C_HEADS + h)),
                      pl.BlockSpec((None, seq // tk, C_VALUE_DIM, tk), lambda b, h, s: (b, 0, h, 0)),
                      pl.BlockSpec((4, C_HEAD_DIM), lambda b, h, s: (0, 0)),
                      pl.BlockSpec((1, C_VALUE_DIM), lambda b, h, s: (0, 0))],
            out_specs=pl.BlockSpec((seq, LANES), lambda b, h, s: (b, h)),
            scratch_shapes=[pltpu.VMEM((2, tk, 2 * tq), F32), pltpu.VMEM((tk, 2 * tq), F32),
                            pltpu.VMEM((1, 2 * tq), F32), pltpu.VMEM((1, 2 * tq), F32),
                            pltpu.VMEM((C_VALUE_DIM, 2 * tq), F32)]),
        compiler_params=_params("parallel", "parallel"),
        name="diff_attn",
    )(slopes, qk, qk, vt, lam_params, sub_gain)


def _post_kernel(*refs, n_mix, layer, fc):
    h_ref = refs[0]
    mix_refs = refs[1:1 + n_mix]
    (w_out_ref, g_mlp_ref, w1_ref, w2_ref, g_ple_ref, w_gate_ref, w_proj_ref, p_ref, o_ref) = refs[1 + n_mix:]
    del layer
    mix = mix_refs[0][...] if n_mix == 1 else jnp.concatenate([m[...] for m in mix_refs], axis=1)
    h = h_ref[...] + _dot(mix, w_out_ref[...])
    hn = _rms(h, g_mlp_ref[...]).astype(BF16)
    acc = h
    for c in range(0, D_FF, fc):
        a = jnp.maximum(_dot(hn, w1_ref[:, c:c + fc]), 0.0)
        acc = acc + _dot((a * a).astype(BF16), w2_ref[c:c + fc, :])
    h = acc
    gate = jax.nn.sigmoid(_dot(_rms(h, g_ple_ref[...]).astype(BF16), w_gate_ref[...]))
    o_ref[...] = h + _dot(p_ref[...].astype(BF16), w_proj_ref[...]) * gate


def _post(h, mixes, w_out, g_mlp, w1, w2, g_ple, w_gate, w_proj, p_all, layer, *, tm=512, fc=512):
    t = h.shape[0]
    row = lambda i: (i, 0)
    mix_specs = [pl.BlockSpec((tm, m.shape[1]), row) for m in mixes]
    return pl.pallas_call(
        functools.partial(_post_kernel, n_mix=len(mixes), layer=layer, fc=fc),
        out_shape=jax.ShapeDtypeStruct((t, D_MODEL), F32),
        grid=(t // tm,),
        in_specs=[pl.BlockSpec((tm, D_MODEL), row)] + mix_specs + [
            _resident((D_MODEL, D_MODEL)), _resident((1, D_MODEL)),
            _resident((D_MODEL, D_FF)), _resident((D_FF, D_MODEL)),
            _resident((1, D_MODEL)), _resident((D_MODEL, D_MODEL)), _resident((PLE_DIM, D_MODEL)),
            pl.BlockSpec((None, tm, PLE_DIM), lambda i: (layer, i, 0))],
        out_specs=pl.BlockSpec((tm, D_MODEL), row),
        compiler_params=_params("parallel"),
        name="post",
    )(h, *mixes, w_out, g_mlp, w1, w2, g_ple, w_gate, w_proj, p_all)


def _lane_row(values, offset):
    return jnp.zeros((1, LANES), F32).at[0, offset:offset + values.shape[0]].set(values.astype(F32))


def kernel(x, p, ln_mix_e, w_in_e, gmlp_v_gain, gmlp_ws, gmlp_bs, gdn_conv, gdn_a_log, gdn_dt_bias,
           gdn_out_gain, w_out_e, ln_mix_o, w_qkv_o, attn_q_gain, attn_k_gain, diff_lambda,
           attn_sub_gain, w_out_o, ln_mlp, w_mlp1, w_mlp2, ln_ple, w_ple_gate, w_ple_proj):
    bsz, seq, d = x.shape
    depth = p.shape[0]
    t = bsz * seq
    h = x.reshape(t, d)
    p_all = p.reshape(depth, t, PLE_DIM)
    slopes = 2.0 ** (-8.0 * jnp.arange(1, C_HEADS + 1, dtype=F32) / C_HEADS)
    row = lambda a: a.reshape(1, -1).astype(F32)
    for layer in range(depth):
        i = layer // 2
        if layer % 2 == 0:
            w_in = jnp.pad(w_in_e[i], ((0, 0), (0, EVEN_PAD - w_in_e.shape[2]))).astype(BF16)
            proj = _in_even(h, row(ln_mix_e[i]), w_in)
            a_out = _gmlp(proj, row(gmlp_v_gain[i]), gmlp_ws[i], gmlp_bs[i].T)
            b_out = _gdn(proj, gdn_conv[i], _lane_row(gdn_a_log[i], B_HEADS),
                         _lane_row(gdn_dt_bias[i], B_HEADS), row(gdn_out_gain[i]), bsz=bsz, seq=seq)
            mixes = (a_out, b_out)
            w_out = w_out_e[i]
        else:
            lam_init = 0.8 - 0.6 * math.exp(-0.3 * layer)
            qg = row(jnp.concatenate([attn_q_gain[i], attn_q_gain[i]]))
            kg = row(jnp.concatenate([attn_k_gain[i], attn_k_gain[i]]))
            qk, vt = _in_odd(h, row(ln_mix_o[i]), w_qkv_o[i].astype(BF16), qg, kg, bsz=bsz, seq=seq,
                             tk=ATTN_BLOCK)
            o = _attn(qk, vt, slopes, diff_lambda[i], row(attn_sub_gain[i]), bsz=bsz, seq=seq,
                      lam_init=lam_init, tq=ATTN_BLOCK)
            mixes = (o,)
            w_out = w_out_o[i]
        h = _post(h, mixes, w_out.astype(BF16), row(ln_mlp[layer]), w_mlp1[layer].astype(BF16),
                  w_mlp2[layer].astype(BF16), row(ln_ple[layer]), w_ple_gate[layer].astype(BF16),
                  w_ple_proj[layer].astype(BF16), p_all, layer)
    return h.reshape(bsz, seq, d)
```

```python
import functools
import math

import jax
import jax.numpy as jnp
from jax import lax
from jax.experimental import pallas as pl
from jax.experimental.pallas import tpu as pltpu

F32 = jnp.float32
BF16 = jnp.bfloat16
NORM_EPS = 1e-6

LANES = 128
SUBLANES = 8
MXU_TILE = 256
D_MODEL = 1024
PLE_DIM = 256
D_FF = 4 * D_MODEL
A_WIDTH = 512
A_GROUPS = 4
A_CHUNK = 128
B_HEADS = 4
B_HEAD_DIM = 128
B_WIDTH = 512
B_CONV = 4
B_CHUNK = 64
CONV_PAD = 8
C_HEADS = 8
C_HEAD_DIM = 64
C_VALUE_DIM = 128
ATTN_BLOCK = 512
ATTN_ROW_CHUNK = 32
LOG2_E = math.log2(math.e)
EVEN_MAIN = 2 * A_WIDTH + 4 * B_WIDTH
EVEN_PAD = EVEN_MAIN + LANES
VMEM_LIMIT = 56 * 1024 * 1024


def _rms(x, gain):
    return x * lax.rsqrt(jnp.mean(x * x, axis=-1, keepdims=True) + NORM_EPS) * gain


def _dot(a, b):
    return jnp.dot(a, b, preferred_element_type=F32)


def _dot_nt(a, b):
    return lax.dot_general(a, b, (((1,), (1,)), ((), ())), preferred_element_type=F32)


def _resident(shape):
    zeros = (0,) * len(shape)
    return pl.BlockSpec(shape, lambda *_: zeros, pipeline_mode=pl.Buffered(1))


def _params(*sem):
    return pltpu.CompilerParams(dimension_semantics=sem, vmem_limit_bytes=VMEM_LIMIT)


def _in_even_kernel(h_ref, g_ref, w_ref, o_ref, *, tn):
    hn = _rms(h_ref[...], g_ref[...]).astype(BF16)
    for start in range(0, EVEN_PAD, tn):
        width = min(tn, EVEN_PAD - start)
        y = _dot(hn, w_ref[:, start:start + width])
        if start < 2 * A_WIDTH:
            y = jax.nn.gelu(y)
        o_ref[:, start:start + width] = y


def _in_even(h, gain, w, *, tm=512, tn=512):
    t = h.shape[0]
    return pl.pallas_call(
        functools.partial(_in_even_kernel, tn=tn),
        out_shape=jax.ShapeDtypeStruct((t, EVEN_PAD), F32),
        grid=(t // tm,),
        in_specs=[pl.BlockSpec((tm, D_MODEL), lambda i: (i, 0)),
                  _resident((1, D_MODEL)), _resident((D_MODEL, EVEN_PAD))],
        out_specs=pl.BlockSpec((tm, EVEN_PAD), lambda i: (i, 0)),
        compiler_params=_params("parallel"),
        name="in_even",
    )(h, gain, w)


def _gmlp_kernel(u_ref, v_ref, gain_ref, ws_ref, bs_ref, o_ref, *, tm):
    ii = lax.broadcasted_iota(jnp.int32, (A_CHUNK, A_CHUNK), 0)
    jj = lax.broadcasted_iota(jnp.int32, (A_CHUNK, A_CHUNK), 1)
    for g in range(A_GROUPS):
        cols = slice(g * LANES, (g + 1) * LANES)
        w = jnp.where(ii >= jj, ws_ref[g], 0.0).astype(BF16)
        bias = bs_ref[:, g:g + 1]
        gain = gain_ref[:, cols]
        for c in range(tm // A_CHUNK):
            rows = slice(c * A_CHUNK, (c + 1) * A_CHUNK)
            v = v_ref[rows, cols]
            vc = v - jnp.mean(v, axis=-1, keepdims=True)
            y = vc * lax.rsqrt(jnp.mean(vc * vc, axis=-1, keepdims=True) + NORM_EPS) * gain
            s = _dot(w, y.astype(BF16)) + bias
            o_ref[rows, cols] = (u_ref[rows, cols] * s).astype(BF16)


def _gmlp(proj, v_gain, ws, bs_t, *, tm=512):
    t = proj.shape[0]
    return pl.pallas_call(
        functools.partial(_gmlp_kernel, tm=tm),
        out_shape=jax.ShapeDtypeStruct((t, A_WIDTH), BF16),
        grid=(t // tm,),
        in_specs=[pl.BlockSpec((tm, A_WIDTH), lambda i: (i, 0)),
                  pl.BlockSpec((tm, A_WIDTH), lambda i: (i, 1)),
                  _resident((1, A_WIDTH)), _resident((A_GROUPS, A_CHUNK, A_CHUNK)),
                  _resident((A_CHUNK, A_GROUPS))],
        out_specs=pl.BlockSpec((tm, A_WIDTH), lambda i: (i, 0)),
        compiler_params=_params("parallel"),
        name="gmlp",
    )(proj, proj, v_gain, ws, bs_t)


def _unit_lower_inverses(a_list, eye, level_masks):
    ds = [eye - jnp.where(level_masks[0], a, 0.0) for a in a_list]
    for mask in level_masks[1:]:
        d16 = [d.astype(BF16) for d in ds]
        mds = [_dot(jnp.where(mask, a, 0.0).astype(BF16), d) for a, d in zip(a_list, d16)]
        ds = [d - _dot(d_lo, md.astype(BF16)) for d, d_lo, md in zip(ds, d16, mds)]
    return ds


def _gdn_kernel(q_ref, k_ref, v_ref, z_ref, ba_ref, cwq_ref, cwk_ref, cwv_ref, alog_ref, dtb_ref,
                og_ref, o_ref, buf_ref, qn_ref, kn_ref, vn_ref, state_ref, wq_ref, u_ref, qk_ref,
                kdt_ref, gl_ref, *, tb, group):
    c_len = B_CHUNK

    @pl.when(pl.program_id(1) == 0)
    def _():
        buf_ref[:, 0:CONV_PAD, :] = jnp.zeros((3, CONV_PAD, B_WIDTH), F32)
        state_ref[...] = jnp.zeros_like(state_ref)

    for idx, (x_ref, cw_ref, dst_ref) in enumerate(
            ((q_ref, cwq_ref, qn_ref), (k_ref, cwk_ref, kn_ref), (v_ref, cwv_ref, vn_ref))):
        x = x_ref[...]
        buf_ref[idx, CONV_PAD:CONV_PAD + tb, :] = x
        cw = cw_ref[...]
        y = cw[B_CONV - 1:B_CONV, :] * x
        for j in range(B_CONV - 1):
            off = CONV_PAD - (B_CONV - 1) + j
            y = y + cw[j:j + 1, :] * buf_ref[idx, off:off + tb, :]
        buf_ref[idx, 0:CONV_PAD, :] = x[tb - CONV_PAD:tb, :]
        y = y * jax.nn.sigmoid(y)
        if idx == 2:
            dst_ref[...] = y
        else:
            scale = B_HEAD_DIM ** -0.5 if idx == 0 else 1.0
            for h in range(B_HEADS):
                cols = slice(h * LANES, (h + 1) * LANES)
                yh = y[:, cols]
                yh = yh * lax.rsqrt(jnp.sum(yh * yh, axis=-1, keepdims=True) + NORM_EPS)
                dst_ref[:, cols] = yh * scale if idx == 0 else yh

    ii = lax.broadcasted_iota(jnp.int32, (c_len, c_len), 0)
    jj = lax.broadcasted_iota(jnp.int32, (c_len, c_len), 1)
    causal = ii >= jj
    strict = ii > jj
    eye = jnp.where(ii == jj, 1.0, 0.0).astype(F32)
    ltri = jnp.where(causal, 1.0, 0.0).astype(BF16)
    level_masks = []
    s = 1
    while s < c_len:
        level_masks.append(((ii // (2 * s)) == (jj // (2 * s))) & ((ii % (2 * s)) >= s) & ((jj % (2 * s)) < s))
        s *= 2
    neg_rate = -jnp.exp(alog_ref[...])
    dt_bias = dtb_ref[...]
    out_gain = og_ref[...]

    heads = range(B_HEADS)

    def prepare(gi, carry):
        prob = []
        for cc in range(group):
            c = gi * group + cc
            rows = pl.ds(pl.multiple_of(c * c_len, c_len), c_len)
            ba = ba_ref[rows, :]
            beta = jax.nn.sigmoid(ba)
            g = neg_rate * jax.nn.softplus(ba + dt_bias)
            g_hi = g.astype(BF16)
            g_lo = (g - g_hi.astype(F32)).astype(BF16)
            gc = _dot(ltri, g_hi) + _dot(ltri, g_lo)
            gc_t = gc.T
            for h in heads:
                cols = slice(h * LANES, (h + 1) * LANES)
                q_h = qn_ref[rows, cols]
                k_h = kn_ref[rows, cols]
                gcol = gc[:, B_HEADS + h:B_HEADS + h + 1]
                grow = gc_t[B_HEADS + h:B_HEADS + h + 1, :]
                glast = gc[c_len - 1:c_len, B_HEADS + h:B_HEADS + h + 1]
                bcol = beta[:, h:h + 1]
                eg = jnp.exp(gcol)
                k_beta = k_h * bcol
                prob.append(dict(
                    c=c, h=h, q_dec=q_h * eg,
                    decay=jnp.where(causal, jnp.exp(jnp.where(causal, gcol - grow, 0.0)), 0.0),
                    lhs=jnp.concatenate([q_h, k_beta], axis=0).astype(BF16), k16=k_h.astype(BF16),
                    rhs=jnp.concatenate([vn_ref[rows, cols] * bcol, k_beta * eg], axis=1).astype(BF16),
                    k_dec=k_h * jnp.exp(glast - gcol), gl=jnp.exp(glast)))
        kks = [_dot_nt(p["lhs"], p["k16"]) for p in prob]
        a_list = [jnp.where(strict, kk[c_len:] * p["decay"], 0.0) for kk, p in zip(kks, prob)]
        t_invs = _unit_lower_inverses(a_list, eye, level_masks)
        uws = [_dot(t.astype(BF16), p["rhs"]) for t, p in zip(t_invs, prob)]
        for p, kk, uw in zip(prob, kks, uws):
            c, h = p["c"], p["h"]
            u_ref[c, h] = uw[:, :B_HEAD_DIM]
            wq_ref[c, h] = jnp.concatenate([uw[:, B_HEAD_DIM:], p["q_dec"]], axis=0).astype(BF16)
            qk_ref[c, h] = (kk[:c_len] * p["decay"]).astype(BF16)
            kdt_ref[c, h] = p["k_dec"].T.astype(BF16)
            gl_ref[c, h] = jnp.broadcast_to(p["gl"], (1, LANES))
        return carry

    lax.fori_loop(0, tb // (c_len * group), prepare, 0)

    def recur(c, carry):
        rows = pl.ds(pl.multiple_of(c * c_len, c_len), c_len)
        states = [state_ref[h] for h in heads]
        wss = [_dot(wq_ref[c, h], states[h].astype(BF16)) for h in heads]
        v_new = [(u_ref[c, h] - wss[h][:c_len]).astype(BF16) for h in heads]
        outs = [wss[h][c_len:] + _dot(qk_ref[c, h], v_new[h]) for h in heads]
        for h in heads:
            state_ref[h] = states[h] * gl_ref[c, h] + _dot(kdt_ref[c, h], v_new[h])
        for h in heads:
            cols = slice(h * LANES, (h + 1) * LANES)
            z = z_ref[rows, cols]
            o_ref[rows, cols] = (_rms(outs[h], out_gain) * (z * jax.nn.sigmoid(z))).astype(BF16)
        return carry

    lax.fori_loop(0, tb // c_len, recur, 0)


def _gdn(proj, conv_w, alog_row, dtb_row, out_gain, *, bsz, seq, tb=512, group=8):
    t = proj.shape[0]
    nt = seq // tb

    def col(j):
        return pl.BlockSpec((tb, B_WIDTH), lambda b, i: (b * nt + i, j))

    def conv(j):
        return pl.BlockSpec((B_CONV, B_WIDTH), lambda b, i: (0, j))

    nc = tb // B_CHUNK
    per_problem = lambda *shape_dtype: pltpu.VMEM((nc, B_HEADS) + shape_dtype[:-1], shape_dtype[-1])
    return pl.pallas_call(
        functools.partial(_gdn_kernel, tb=tb, group=group),
        out_shape=jax.ShapeDtypeStruct((t, B_WIDTH), BF16),
        grid=(bsz, nt),
        in_specs=[col(2), col(3), col(4), col(5),
                  pl.BlockSpec((tb, LANES), lambda b, i: (b * nt + i, EVEN_MAIN // LANES)),
                  conv(0), conv(1), conv(2),
                  pl.BlockSpec((1, LANES), lambda b, i: (0, 0)),
                  pl.BlockSpec((1, LANES), lambda b, i: (0, 0)),
                  pl.BlockSpec((1, B_HEAD_DIM), lambda b, i: (0, 0))],
        out_specs=pl.BlockSpec((tb, B_WIDTH), lambda b, i: (b * nt + i, 0)),
        scratch_shapes=[pltpu.VMEM((3, CONV_PAD + tb, B_WIDTH), F32),
                        pltpu.VMEM((tb, B_WIDTH), F32), pltpu.VMEM((tb, B_WIDTH), F32),
                        pltpu.VMEM((tb, B_WIDTH), F32),
                        pltpu.VMEM((B_HEADS, B_HEAD_DIM, B_HEAD_DIM), F32),
                        per_problem(2 * B_CHUNK, B_HEAD_DIM, BF16),
                        per_problem(B_CHUNK, B_HEAD_DIM, F32),
                        per_problem(B_CHUNK, B_CHUNK, BF16),
                        per_problem(B_HEAD_DIM, B_CHUNK, BF16),
                        per_problem(1, LANES, F32)],
        compiler_params=_params("parallel", "arbitrary"),
        name="gdn",
    )(proj, proj, proj, proj, proj, conv_w, conv_w, conv_w, alog_row, dtb_row, out_gain)


def _in_odd_kernel(h_ref, g_ref, w_ref, qg_ref, kg_ref, o_ref, vt_ref, *, tm, tn, tk):
    hn = _rms(h_ref[...], g_ref[...]).astype(BF16)
    lane = lax.broadcasted_iota(jnp.int32, (tm, LANES), 1)
    low = lane < C_HEAD_DIM
    qk_width = 2 * C_HEADS * 2 * C_HEAD_DIM
    for start in range(0, 3 * C_HEADS * C_VALUE_DIM, tn):
        y = _dot(hn, w_ref[:, start:start + tn])
        if start >= qk_width:
            for r0 in range(0, tm, MXU_TILE):
                vt_ref[r0 // tk, start - qk_width:start - qk_width + tn, r0 % tk:r0 % tk + MXU_TILE] = (
                    y[r0:r0 + MXU_TILE, :].T.astype(BF16))
            continue
        is_q = start < qk_width // 2
        gain = qg_ref[...] if is_q else kg_ref[...]
        for j in range(tn // LANES):
            yb = y[:, j * LANES:(j + 1) * LANES]
            y2 = yb * yb
            lo = jnp.sum(jnp.where(low, y2, 0.0), axis=-1, keepdims=True)
            hi = jnp.sum(jnp.where(low, 0.0, y2), axis=-1, keepdims=True)
            ms = jnp.where(low, lo, hi) * (1.0 / C_HEAD_DIM)
            yn = yb * lax.rsqrt(ms + NORM_EPS) * gain
            if is_q:
                yn = yn * (C_HEAD_DIM ** -0.5 * LOG2_E)
            o_ref[:, start + j * LANES:start + (j + 1) * LANES] = yn.astype(BF16)


def _in_odd(h, gain, w, qg, kg, *, bsz, seq, tk, tm=512, tn=512):
    t = h.shape[0]
    n = w.shape[1]
    nt = seq // tm
    qk_width = 2 * C_HEADS * 2 * C_HEAD_DIM
    return pl.pallas_call(
        functools.partial(_in_odd_kernel, tm=tm, tn=tn, tk=tk),
        out_shape=(jax.ShapeDtypeStruct((t, qk_width), BF16),
                   jax.ShapeDtypeStruct((bsz, seq // tk, n - qk_width, tk), BF16)),
        grid=(t // tm,),
        in_specs=[pl.BlockSpec((tm, D_MODEL), lambda i: (i, 0)),
                  _resident((1, D_MODEL)), _resident((D_MODEL, n)),
                  _resident((1, LANES)), _resident((1, LANES))],
        out_specs=(pl.BlockSpec((tm, qk_width), lambda i: (i, 0)),
                   pl.BlockSpec((None, tm // tk, n - qk_width, tk), lambda i: (i // nt, i % nt, 0, 0))),
        compiler_params=_params("parallel"),
        name="in_odd",
    )(h, gain, w, qg, kg)


def _attn_kernel(slopes_ref, q_ref, k_ref, vt_ref, lam_ref, sg_ref, o_ref, s0_ref, s1_ref, p0_ref, p1_ref,
                 q2_ref, rel_ref, m_ref, l_ref, alpha_ref, acc_ref, *, tq, tk, lam_init):
    slope = slopes_ref[pl.program_id(1)] * LOG2_E
    nq = q_ref.shape[0] // tq
    ki = lax.broadcasted_iota(jnp.int32, (tk, 2 * tq), 0)
    qj = lax.broadcasted_iota(jnp.int32, (tk, 2 * tq), 1)
    qj = jnp.where(qj >= tq, qj - tq, qj)
    rel_ref[...] = (ki - qj).astype(F32) * slope
    lane = lax.broadcasted_iota(jnp.int32, (tq, LANES), 1)
    lp = lam_ref[...]
    lam = (jnp.exp(jnp.sum(lp[0:1] * lp[1:2], axis=-1, keepdims=True))
           - jnp.exp(jnp.sum(lp[2:3] * lp[3:4], axis=-1, keepdims=True)) + lam_init)
    sub_gain = sg_ref[...]

    def load_q2(qi):
        q = q_ref[pl.ds(pl.multiple_of(qi * tq, tq), tq), :]
        zero = jnp.zeros_like(q)
        return jnp.concatenate([jnp.where(lane < C_HEAD_DIM, q, zero),
                                jnp.where(lane < C_HEAD_DIM, zero, q)], axis=0)

    lanes2 = 2 * tq
    s_bufs = (s0_ref, s1_ref)
    p_bufs = (p0_ref, p1_ref)
    chunk = ATTN_ROW_CHUNK
    sub = chunk // SUBLANES

    def fold(x, op):
        return op(x.reshape(sub, SUBLANES, lanes2), axis=0)

    def block_max(par, masked, r0, mx):
        t = s_bufs[par][r0:r0 + chunk, :]
        if masked:
            krow = lax.broadcasted_iota(jnp.int32, (chunk, lanes2), 0) + r0
            qcol = lax.broadcasted_iota(jnp.int32, (chunk, lanes2), 1)
            qcol = jnp.where(qcol >= tq, qcol - tq, qcol)
            t = jnp.where(krow <= qcol, t, -jnp.inf)
            s_bufs[par][r0:r0 + chunk, :] = t
        return jnp.maximum(mx, fold(t, jnp.max))

    def block_exp(par, shift, r0, total):
        p = jnp.exp2(s_bufs[par][r0:r0 + chunk, :] + shift)
        p_bufs[par][r0:r0 + chunk, :] = p.astype(BF16)
        return total + fold(p, jnp.sum)

    q2_first = load_q2(0)
    q2_ref[...] = q2_first
    s_bufs[0][...] = _dot_nt(k_ref[0:tk, :], q2_first) + rel_ref[...]

    def q_block(qi, base):
        q0 = pl.multiple_of(qi * tq, tq)
        q2 = q2_ref[...]
        m_ref[...] = jnp.full(m_ref.shape, -jnp.inf, F32)
        l_ref[...] = jnp.zeros(l_ref.shape, F32)
        acc_ref[...] = jnp.zeros(acc_ref.shape, F32)
        alpha_ref[...] = jnp.ones(alpha_ref.shape, F32)
        for par in range(2):
            @pl.when(base == par)
            def _():
                p_bufs[1 - par][...] = jnp.zeros(p_bufs[0].shape, BF16)

        def step(kb, last, par):
            if last:
                q2_next = load_q2(jnp.minimum(qi + 1, nq - 1))
                q2_ref[...] = q2_next
                k_next = k_ref[0:tk, :]
            else:
                q2_next = q2
                k_next = k_ref[pl.ds(pl.multiple_of((kb + 1) * tk, tk), tk), :]
            vt_prev = vt_ref[jnp.maximum(kb - 1, 0)]
            alpha_prev = alpha_ref[...]

            def scores_piece(j):
                cols = slice(j * MXU_TILE, (j + 1) * MXU_TILE)
                s_bufs[1 - par][:, cols] = _dot_nt(k_next, q2_next[cols, :]) + rel_ref[:, cols]

            def pv_piece(j):
                cols = slice(j * MXU_TILE, (j + 1) * MXU_TILE)
                acc_ref[:, cols] = (alpha_prev[:, cols] * acc_ref[:, cols]
                                    + _dot(vt_prev, p_bufs[1 - par][:, cols]))

            n_piece = lanes2 // MXU_TILE
            rows = list(range(0, tk, chunk))
            mx = jnp.full((SUBLANES, lanes2), -jnp.inf, F32)
            for i, r0 in enumerate(rows):
                if i % (len(rows) // 2) == 0:
                    scores_piece(i // (len(rows) // 2))
                mx = block_max(par, last, r0, mx)
            offset = slope * (kb * tk - q0).astype(F32)
            m_prev = m_ref[...]
            m_new = jnp.maximum(m_prev, jnp.max(mx, axis=0, keepdims=True) + offset)
            alpha = jnp.exp2(m_prev - m_new)
            shift = offset - m_new
            pieces = ([functools.partial(scores_piece, j) for j in range(2, n_piece)]
                      + [functools.partial(pv_piece, j) for j in range(n_piece)])
            every = len(rows) // len(pieces)
            total = jnp.zeros((SUBLANES, lanes2), F32)
            for i, r0 in enumerate(rows):
                if i % every == 0 and i // every < len(pieces):
                    pieces[i // every]()
                total = block_exp(par, shift, r0, total)
            for piece in pieces[len(rows) // every:]:
                piece()
            l_ref[...] = alpha * l_ref[...] + jnp.sum(total, axis=0, keepdims=True)
            if last:
                acc_ref[...] = alpha * acc_ref[...] + _dot(vt_ref[kb], p_bufs[par][...])
            alpha_ref[...] = alpha
            m_ref[...] = m_new

        def either_parity(kb, last):
            dyn_par = (base + kb) % 2
            for par in range(2):
                pl.when(dyn_par == par)(functools.partial(step, kb, last, par))

        def body(kb, c):
            either_parity(kb, False)
            return c

        lax.fori_loop(0, qi, body, 0)
        either_parity(qi, True)
        o2 = acc_ref[...] / l_ref[...]
        o_t = o2[:, :tq] - lam * o2[:, tq:]
        o_ref[pl.ds(q0, tq), :] = (_rms(o_t.T, sub_gain) * (1.0 - lam_init)).astype(BF16)
        return (base + qi + 1) % 2

    lax.fori_loop(0, nq, q_block, jnp.int32(0))


def _attn(qk, vt, slopes, lam_params, sub_gain, *, bsz, seq, lam_init, tq):
    t = qk.shape[0]
    tk = vt.shape[3]
    assert tk == tq
    kernel = functools.partial(_attn_kernel, tq=tq, tk=tk, lam_init=lam_init)
    return pl.pallas_call(
        kernel,
        out_shape=jax.ShapeDtypeStruct((t, C_HEADS * C_VALUE_DIM), BF16),
        grid_spec=pltpu.PrefetchScalarGridSpec(
            num_scalar_prefetch=1,
            grid=(bsz, C_HEADS),
            in_specs=[pl.BlockSpec((seq, LANES), lambda b, h, s: (b, h)),
                      pl.BlockSpec((seq, LANES), lambda b, h, s: (b, C_HEADS + h)),
                      pl.BlockSpec((None, seq // tk, C_VALUE_DIM, tk), lambda b, h, s: (b, 0, h, 0)),
                      pl.BlockSpec((4, C_HEAD_DIM), lambda b, h, s: (0, 0)),
                      pl.BlockSpec((1, C_VALUE_DIM), lambda b, h, s: (0, 0))],
            out_specs=pl.BlockSpec((seq, LANES), lambda b, h, s: (b, h)),
            scratch_shapes=[pltpu.VMEM((tk, 2 * tq), F32), pltpu.VMEM((tk, 2 * tq), F32),
                            pltpu.VMEM((tk, 2 * tq), BF16), pltpu.VMEM((tk, 2 * tq), BF16),
                            pltpu.VMEM((2 * tq, LANES), BF16),
                            pltpu.VMEM((tk, 2 * tq), F32),
                            pltpu.VMEM((1, 2 * tq), F32), pltpu.VMEM((1, 2 * tq), F32),
                            pltpu.VMEM((1, 2 * tq), F32),
                            pltpu.VMEM((C_VALUE_DIM, 2 * tq), F32)]),
        compiler_params=_params("parallel", "parallel"),
        name="diff_attn",
    )(slopes, qk, qk, vt, lam_params, sub_gain)


def _post_kernel(*refs, n_mix, layer, fc):
    h_ref = refs[0]
    mix_refs = refs[1:1 + n_mix]
    (w_out_ref, g_mlp_ref, w1_ref, w2_ref, g_ple_ref, w_gate_ref, w_proj_ref, p_ref, o_ref) = refs[1 + n_mix:]
    del layer
    mix = mix_refs[0][...] if n_mix == 1 else jnp.concatenate([m[...] for m in mix_refs], axis=1)
    h = h_ref[...] + _dot(mix, w_out_ref[...])
    hn = _rms(h, g_mlp_ref[...]).astype(BF16)
    acc = h
    for c in range(0, D_FF, fc):
        a = jnp.maximum(_dot(hn, w1_ref[:, c:c + fc]), 0.0)
        acc = acc + _dot((a * a).astype(BF16), w2_ref[c:c + fc, :])
    h = acc
    gate = jax.nn.sigmoid(_dot(_rms(h, g_ple_ref[...]).astype(BF16), w_gate_ref[...]))
    o_ref[...] = h + _dot(p_ref[...].astype(BF16), w_proj_ref[...]) * gate


def _post(h, mixes, w_out, g_mlp, w1, w2, g_ple, w_gate, w_proj, p_all, layer, *, tm=512, fc=512):
    t = h.shape[0]
    row = lambda i: (i, 0)
    mix_specs = [pl.BlockSpec((tm, m.shape[1]), row) for m in mixes]
    return pl.pallas_call(
        functools.partial(_post_kernel, n_mix=len(mixes), layer=layer, fc=fc),
        out_shape=jax.ShapeDtypeStruct((t, D_MODEL), F32),
        grid=(t // tm,),
        in_specs=[pl.BlockSpec((tm, D_MODEL), row)] + mix_specs + [
            _resident((D_MODEL, D_MODEL)), _resident((1, D_MODEL)),
            _resident((D_MODEL, D_FF)), _resident((D_FF, D_MODEL)),
            _resident((1, D_MODEL)), _resident((D_MODEL, D_MODEL)), _resident((PLE_DIM, D_MODEL)),
            pl.BlockSpec((None, tm, PLE_DIM), lambda i: (layer, i, 0))],
        out_specs=pl.BlockSpec((tm, D_MODEL), row),
        compiler_params=_params("parallel"),
        name="post",
    )(h, *mixes, w_out, g_mlp, w1, w2, g_ple, w_gate, w_proj, p_all)


def _lane_row(values, offset):
    return jnp.zeros((1, LANES), F32).at[0, offset:offset + values.shape[0]].set(values.astype(F32))


def kernel(x, p, ln_mix_e, w_in_e, gmlp_v_gain, gmlp_ws, gmlp_bs, gdn_conv, gdn_a_log, gdn_dt_bias,
           gdn_out_gain, w_out_e, ln_mix_o, w_qkv_o, attn_q_gain, attn_k_gain, diff_lambda,
           attn_sub_gain, w_out_o, ln_mlp, w_mlp1, w_mlp2, ln_ple, w_ple_gate, w_ple_proj):
    bsz, seq, d = x.shape
    depth = p.shape[0]
    t = bsz * seq
    h = x.reshape(t, d)
    p_all = p.reshape(depth, t, PLE_DIM)
    slopes = 2.0 ** (-8.0 * jnp.arange(1, C_HEADS + 1, dtype=F32) / C_HEADS)
    row = lambda a: a.reshape(1, -1).astype(F32)
    for layer in range(depth):
        i = layer // 2
        if layer % 2 == 0:
            w_in = jnp.pad(w_in_e[i], ((0, 0), (0, EVEN_PAD - w_in_e.shape[2]))).astype(BF16)
            proj = _in_even(h, row(ln_mix_e[i]), w_in)
            a_out = _gmlp(proj, row(gmlp_v_gain[i]), gmlp_ws[i], gmlp_bs[i].T)
            b_out = _gdn(proj, gdn_conv[i], _lane_row(gdn_a_log[i], B_HEADS),
                         _lane_row(gdn_dt_bias[i], B_HEADS), row(gdn_out_gain[i]), bsz=bsz, seq=seq)
            mixes = (a_out, b_out)
            w_out = w_out_e[i]
        else:
            lam_init = 0.8 - 0.6 * math.exp(-0.3 * layer)
            qg = row(jnp.concatenate([attn_q_gain[i], attn_q_gain[i]]))
            kg = row(jnp.concatenate([attn_k_gain[i], attn_k_gain[i]]))
            qk, vt = _in_odd(h, row(ln_mix_o[i]), w_qkv_o[i].astype(BF16), qg, kg, bsz=bsz, seq=seq,
                             tk=ATTN_BLOCK)
            o = _attn(qk, vt, slopes, diff_lambda[i], row(attn_sub_gain[i]), bsz=bsz, seq=seq,
                      lam_init=lam_init, tq=ATTN_BLOCK)
            mixes = (o,)
            w_out = w_out_o[i]
        h = _post(h, mixes, w_out.astype(BF16), row(ln_mlp[layer]), w_mlp1[layer].astype(BF16),
                  w_mlp2[layer].astype(BF16), row(ln_ple[layer]), w_ple_gate[layer].astype(BF16),
                  w_ple_proj[layer].astype(BF16), p_all, layer)
    return h.reshape(bsz, seq, d)
```

```python
import functools
import math

import jax
import jax.numpy as jnp
from jax import lax
from jax.experimental import pallas as pl
from jax.experimental.pallas import tpu as pltpu

F32 = jnp.float32
BF16 = jnp.bfloat16
NORM_EPS = 1e-6

LANES = 128
SUBLANES = 8
MXU_TILE = 256
D_MODEL = 1024
PLE_DIM = 256
D_FF = 4 * D_MODEL
A_WIDTH = 512
A_GROUPS = 4
A_CHUNK = 128
B_HEADS = 4
B_HEAD_DIM = 128
B_WIDTH = 512
B_CONV = 4
B_CHUNK = 64
CONV_PAD = 8
C_HEADS = 8
C_HEAD_DIM = 64
C_VALUE_DIM = 128
ATTN_BLOCK = 512
ATTN_ROW_CHUNK = 32
LOG2_E = math.log2(math.e)
EVEN_MAIN = 2 * A_WIDTH + 4 * B_WIDTH
EVEN_PAD = EVEN_MAIN + LANES
VMEM_LIMIT = 56 * 1024 * 1024


def _rms(x, gain):
    return x * lax.rsqrt(jnp.mean(x * x, axis=-1, keepdims=True) + NORM_EPS) * gain


def _dot(a, b):
    return jnp.dot(a, b, preferred_element_type=F32)


def _dot_nt(a, b):
    return lax.dot_general(a, b, (((1,), (1,)), ((), ())), preferred_element_type=F32)


def _resident(shape):
    zeros = (0,) * len(shape)
    return pl.BlockSpec(shape, lambda *_: zeros, pipeline_mode=pl.Buffered(1))


def _params(*sem):
    return pltpu.CompilerParams(dimension_semantics=sem, vmem_limit_bytes=VMEM_LIMIT)


def _in_even_kernel(h_ref, g_ref, w_ref, o_ref, *, tn):
    hn = _rms(h_ref[...], g_ref[...]).astype(BF16)
    for start in range(0, EVEN_PAD, tn):
        width = min(tn, EVEN_PAD - start)
        y = _dot(hn, w_ref[:, start:start + width])
        if start < 2 * A_WIDTH:
            y = jax.nn.gelu(y)
        o_ref[:, start:start + width] = y


def _in_even(h, gain, w, *, tm=512, tn=512):
    t = h.shape[0]
    return pl.pallas_call(
        functools.partial(_in_even_kernel, tn=tn),
        out_shape=jax.ShapeDtypeStruct((t, EVEN_PAD), F32),
        grid=(t // tm,),
        in_specs=[pl.BlockSpec((tm, D_MODEL), lambda i: (i, 0)),
                  _resident((1, D_MODEL)), _resident((D_MODEL, EVEN_PAD))],
        out_specs=pl.BlockSpec((tm, EVEN_PAD), lambda i: (i, 0)),
        compiler_params=_params("parallel"),
        name="in_even",
    )(h, gain, w)


def _gmlp_kernel(u_ref, v_ref, gain_ref, ws_ref, bs_ref, o_ref, *, tm):
    ii = lax.broadcasted_iota(jnp.int32, (A_CHUNK, A_CHUNK), 0)
    jj = lax.broadcasted_iota(jnp.int32, (A_CHUNK, A_CHUNK), 1)
    for g in range(A_GROUPS):
        cols = slice(g * LANES, (g + 1) * LANES)
        w = jnp.where(ii >= jj, ws_ref[g], 0.0).astype(BF16)
        bias = bs_ref[:, g:g + 1]
        gain = gain_ref[:, cols]
        for c in range(tm // A_CHUNK):
            rows = slice(c * A_CHUNK, (c + 1) * A_CHUNK)
            v = v_ref[rows, cols]
            vc = v - jnp.mean(v, axis=-1, keepdims=True)
            y = vc * lax.rsqrt(jnp.mean(vc * vc, axis=-1, keepdims=True) + NORM_EPS) * gain
            s = _dot(w, y.astype(BF16)) + bias
            o_ref[rows, cols] = (u_ref[rows, cols] * s).astype(BF16)


def _gmlp(proj, v_gain, ws, bs_t, *, tm=512):
    t = proj.shape[0]
    return pl.pallas_call(
        functools.partial(_gmlp_kernel, tm=tm),
        out_shape=jax.ShapeDtypeStruct((t, A_WIDTH), BF16),
        grid=(t // tm,),
        in_specs=[pl.BlockSpec((tm, A_WIDTH), lambda i: (i, 0)),
                  pl.BlockSpec((tm, A_WIDTH), lambda i: (i, 1)),
                  _resident((1, A_WIDTH)), _resident((A_GROUPS, A_CHUNK, A_CHUNK)),
                  _resident((A_CHUNK, A_GROUPS))],
        out_specs=pl.BlockSpec((tm, A_WIDTH), lambda i: (i, 0)),
        compiler_params=_params("parallel"),
        name="gmlp",
    )(proj, proj, v_gain, ws, bs_t)


def _unit_lower_inverses(a_list, eye, level_masks):
    ds = [eye - jnp.where(level_masks[0], a, 0.0) for a in a_list]
    for mask in level_masks[1:]:
        d16 = [d.astype(BF16) for d in ds]
        mds = [_dot(jnp.where(mask, a, 0.0).astype(BF16), d) for a, d in zip(a_list, d16)]
        ds = [d - _dot(d_lo, md.astype(BF16)) for d, d_lo, md in zip(ds, d16, mds)]
    return ds


def _gdn_kernel(q_ref, k_ref, v_ref, z_ref, ba_ref, cwq_ref, cwk_ref, cwv_ref, alog_ref, dtb_ref,
                og_ref, o_ref, buf_ref, qn_ref, kn_ref, vn_ref, state_ref, lhs_ref, ds_ref, o0_ref,
                gl_ref, raw_ref, *, tb, group):
    c_len = B_CHUNK

    @pl.when(pl.program_id(1) == 0)
    def _():
        buf_ref[:, 0:CONV_PAD, :] = jnp.zeros((3, CONV_PAD, B_WIDTH), F32)
        state_ref[...] = jnp.zeros_like(state_ref)

    for idx, (x_ref, cw_ref, dst_ref) in enumerate(
            ((q_ref, cwq_ref, qn_ref), (k_ref, cwk_ref, kn_ref), (v_ref, cwv_ref, vn_ref))):
        x = x_ref[...]
        buf_ref[idx, CONV_PAD:CONV_PAD + tb, :] = x
        cw = cw_ref[...]
        y = cw[B_CONV - 1:B_CONV, :] * x
        for j in range(B_CONV - 1):
            off = CONV_PAD - (B_CONV - 1) + j
            y = y + cw[j:j + 1, :] * buf_ref[idx, off:off + tb, :]
        buf_ref[idx, 0:CONV_PAD, :] = x[tb - CONV_PAD:tb, :]
        y = y * jax.nn.sigmoid(y)
        if idx == 2:
            dst_ref[...] = y
        else:
            scale = B_HEAD_DIM ** -0.5 if idx == 0 else 1.0
            for h in range(B_HEADS):
                cols = slice(h * LANES, (h + 1) * LANES)
                yh = y[:, cols]
                yh = yh * lax.rsqrt(jnp.sum(yh * yh, axis=-1, keepdims=True) + NORM_EPS)
                dst_ref[:, cols] = yh * scale if idx == 0 else yh

    ii = lax.broadcasted_iota(jnp.int32, (c_len, c_len), 0)
    jj = lax.broadcasted_iota(jnp.int32, (c_len, c_len), 1)
    causal = ii >= jj
    strict = ii > jj
    eye = jnp.where(ii == jj, 1.0, 0.0).astype(F32)
    ltri = jnp.where(causal, 1.0, 0.0).astype(BF16)
    level_masks = []
    s = 1
    while s < c_len:
        level_masks.append(((ii // (2 * s)) == (jj // (2 * s))) & ((ii % (2 * s)) >= s) & ((jj % (2 * s)) < s))
        s *= 2
    neg_rate = -jnp.exp(alog_ref[...])
    dt_bias = dtb_ref[...]
    out_gain = og_ref[...]

    heads = range(B_HEADS)

    def prepare(gi, carry):
        prob = []
        for cc in range(group):
            c = gi * group + cc
            rows = pl.ds(pl.multiple_of(c * c_len, c_len), c_len)
            ba = ba_ref[rows, :]
            beta = jax.nn.sigmoid(ba)
            g = neg_rate * jax.nn.softplus(ba + dt_bias)
            g_hi = g.astype(BF16)
            g_lo = (g - g_hi.astype(F32)).astype(BF16)
            gc = _dot(ltri, g_hi) + _dot(ltri, g_lo)
            gc_t = gc.T
            for h in heads:
                cols = slice(h * LANES, (h + 1) * LANES)
                q_h = qn_ref[rows, cols]
                k_h = kn_ref[rows, cols]
                gcol = gc[:, B_HEADS + h:B_HEADS + h + 1]
                grow = gc_t[B_HEADS + h:B_HEADS + h + 1, :]
                glast = gc[c_len - 1:c_len, B_HEADS + h:B_HEADS + h + 1]
                bcol = beta[:, h:h + 1]
                eg = jnp.exp(gcol)
                k_beta = k_h * bcol
                prob.append(dict(
                    c=c, h=h, q_dec=q_h * eg,
                    decay=jnp.where(causal, jnp.exp(jnp.where(causal, gcol - grow, 0.0)), 0.0),
                    lhs=jnp.concatenate([q_h, k_beta], axis=0).astype(BF16), k16=k_h.astype(BF16),
                    rhs=jnp.concatenate([k_beta * eg, vn_ref[rows, cols] * bcol], axis=1).astype(BF16),
                    k_dec=k_h * jnp.exp(glast - gcol), gl=jnp.exp(glast)))
        kks = [_dot_nt(p["lhs"], p["k16"]) for p in prob]
        a_list = [jnp.where(strict, kk[c_len:] * p["decay"], 0.0) for kk, p in zip(kks, prob)]
        t_invs = _unit_lower_inverses(a_list, eye, level_masks)
        wus = [_dot(t.astype(BF16), p["rhs"]).astype(BF16) for t, p in zip(t_invs, prob)]
        kd_wu = [_dot(p["k_dec"].T.astype(BF16), wu) for p, wu in zip(prob, wus)]
        qk_wu = [_dot((kk[:c_len] * p["decay"]).astype(BF16), wu) for p, kk, wu in zip(prob, kks, wus)]
        for p, kd, qk in zip(prob, kd_wu, qk_wu):
            c, h = p["c"], p["h"]
            lhs_ref[c, h] = jnp.concatenate(
                [kd[:, :B_HEAD_DIM], qk[:, :B_HEAD_DIM] - p["q_dec"]], axis=0).astype(BF16)
            ds_ref[c, h] = kd[:, B_HEAD_DIM:]
            o0_ref[c, h] = qk[:, B_HEAD_DIM:]
            gl_ref[c, h] = jnp.broadcast_to(p["gl"], (1, LANES))
        return carry

    lax.fori_loop(0, tb // (c_len * group), prepare, 0)

    def recur(c, carry):
        rows = pl.ds(pl.multiple_of(c * c_len, c_len), c_len)
        states = [state_ref[h] for h in heads]
        prods = [_dot(lhs_ref[c, h], states[h].astype(BF16)) for h in heads]
        for h in heads:
            state_ref[h] = states[h] * gl_ref[c, h] + (ds_ref[c, h] - prods[h][:B_HEAD_DIM])
            raw_ref[rows, h * LANES:(h + 1) * LANES] = o0_ref[c, h] - prods[h][B_HEAD_DIM:]
        return carry

    lax.fori_loop(0, tb // c_len, recur, 0)

    for h in heads:
        cols = slice(h * LANES, (h + 1) * LANES)
        z = z_ref[:, cols]
        o_ref[:, cols] = (_rms(raw_ref[:, cols], out_gain) * (z * jax.nn.sigmoid(z))).astype(BF16)


def _gdn(proj, conv_w, alog_row, dtb_row, out_gain, *, bsz, seq, tb=512, group=8):
    t = proj.shape[0]
    nt = seq // tb

    def col(j):
        return pl.BlockSpec((tb, B_WIDTH), lambda b, i: (b * nt + i, j))

    def conv(j):
        return pl.BlockSpec((B_CONV, B_WIDTH), lambda b, i: (0, j))

    nc = tb // B_CHUNK
    per_problem = lambda *shape_dtype: pltpu.VMEM((nc, B_HEADS) + shape_dtype[:-1], shape_dtype[-1])
    return pl.pallas_call(
        functools.partial(_gdn_kernel, tb=tb, group=group),
        out_shape=jax.ShapeDtypeStruct((t, B_WIDTH), BF16),
        grid=(bsz, nt),
        in_specs=[col(2), col(3), col(4), col(5),
                  pl.BlockSpec((tb, LANES), lambda b, i: (b * nt + i, EVEN_MAIN // LANES)),
                  conv(0), conv(1), conv(2),
                  pl.BlockSpec((1, LANES), lambda b, i: (0, 0)),
                  pl.BlockSpec((1, LANES), lambda b, i: (0, 0)),
                  pl.BlockSpec((1, B_HEAD_DIM), lambda b, i: (0, 0))],
        out_specs=pl.BlockSpec((tb, B_WIDTH), lambda b, i: (b * nt + i, 0)),
        scratch_shapes=[pltpu.VMEM((3, CONV_PAD + tb, B_WIDTH), F32),
                        pltpu.VMEM((tb, B_WIDTH), F32), pltpu.VMEM((tb, B_WIDTH), F32),
                        pltpu.VMEM((tb, B_WIDTH), F32),
                        pltpu.VMEM((B_HEADS, B_HEAD_DIM, B_HEAD_DIM), F32),
                        per_problem(B_HEAD_DIM + B_CHUNK, B_HEAD_DIM, BF16),
                        per_problem(B_HEAD_DIM, B_HEAD_DIM, F32),
                        per_problem(B_CHUNK, B_HEAD_DIM, F32),
                        per_problem(1, LANES, F32),
                        pltpu.VMEM((tb, B_WIDTH), F32)],
        compiler_params=_params("parallel", "arbitrary"),
        name="gdn",
    )(proj, proj, proj, proj, proj, conv_w, conv_w, conv_w, alog_row, dtb_row, out_gain)


def _in_odd_kernel(h_ref, g_ref, w_ref, qg_ref, kg_ref, o_ref, vt_ref, *, tm, tn, tk):
    hn = _rms(h_ref[...], g_ref[...]).astype(BF16)
    lane = lax.broadcasted_iota(jnp.int32, (tm, LANES), 1)
    low = lane < C_HEAD_DIM
    qk_width = 2 * C_HEADS * 2 * C_HEAD_DIM
    for start in range(0, 3 * C_HEADS * C_VALUE_DIM, tn):
        y = _dot(hn, w_ref[:, start:start + tn])
        if start >= qk_width:
            for r0 in range(0, tm, MXU_TILE):
                vt_ref[r0 // tk, start - qk_width:start - qk_width + tn, r0 % tk:r0 % tk + MXU_TILE] = (
                    y[r0:r0 + MXU_TILE, :].T.astype(BF16))
            continue
        is_q = start < qk_width // 2
        gain = qg_ref[...] if is_q else kg_ref[...]
        for j in range(tn // LANES):
            yb = y[:, j * LANES:(j + 1) * LANES]
            y2 = yb * yb
            lo = jnp.sum(jnp.where(low, y2, 0.0), axis=-1, keepdims=True)
            hi = jnp.sum(jnp.where(low, 0.0, y2), axis=-1, keepdims=True)
            ms = jnp.where(low, lo, hi) * (1.0 / C_HEAD_DIM)
            yn = yb * lax.rsqrt(ms + NORM_EPS) * gain
            if is_q:
                yn = yn * (C_HEAD_DIM ** -0.5 * LOG2_E)
            o_ref[:, start + j * LANES:start + (j + 1) * LANES] = yn.astype(BF16)


def _in_odd(h, gain, w, qg, kg, *, bsz, seq, tk, tm=512, tn=512):
    t = h.shape[0]
    n = w.shape[1]
    nt = seq // tm
    qk_width = 2 * C_HEADS * 2 * C_HEAD_DIM
    return pl.pallas_call(
        functools.partial(_in_odd_kernel, tm=tm, tn=tn, tk=tk),
        out_shape=(jax.ShapeDtypeStruct((t, qk_width), BF16),
                   jax.ShapeDtypeStruct((bsz, seq // tk, n - qk_width, tk), BF16)),
        grid=(t // tm,),
        in_specs=[pl.BlockSpec((tm, D_MODEL), lambda i: (i, 0)),
                  _resident((1, D_MODEL)), _resident((D_MODEL, n)),
                  _resident((1, LANES)), _resident((1, LANES))],
        out_specs=(pl.BlockSpec((tm, qk_width), lambda i: (i, 0)),
                   pl.BlockSpec((None, tm // tk, n - qk_width, tk), lambda i: (i // nt, i % nt, 0, 0))),
        compiler_params=_params("parallel"),
        name="in_odd",
    )(h, gain, w, qg, kg)


def _attn_kernel(slopes_ref, q_ref, k_ref, vt_ref, lam_ref, sg_ref, o_ref, s0_ref, s1_ref, mx0_ref, mx1_ref,
                 p_ref, q2_ref, rel_ref, m_ref, l_ref, acc_ref, *, tq, tk, lam_init):
    slope = slopes_ref[pl.program_id(1)] * LOG2_E
    nq = q_ref.shape[0] // tq
    ki = lax.broadcasted_iota(jnp.int32, (tk, 2 * tq), 0)
    qj = lax.broadcasted_iota(jnp.int32, (tk, 2 * tq), 1)
    qj = jnp.where(qj >= tq, qj - tq, qj)
    rel_ref[...] = (ki - qj).astype(F32) * slope
    lane = lax.broadcasted_iota(jnp.int32, (tq, LANES), 1)
    lp = lam_ref[...]
    lam = (jnp.exp(jnp.sum(lp[0:1] * lp[1:2], axis=-1, keepdims=True))
           - jnp.exp(jnp.sum(lp[2:3] * lp[3:4], axis=-1, keepdims=True)) + lam_init)
    sub_gain = sg_ref[...]

    def load_q2(qi):
        q = q_ref[pl.ds(pl.multiple_of(qi * tq, tq), tq), :]
        zero = jnp.zeros_like(q)
        return jnp.concatenate([jnp.where(lane < C_HEAD_DIM, q, zero),
                                jnp.where(lane < C_HEAD_DIM, zero, q)], axis=0)

    lanes2 = 2 * tq
    s_bufs = (s0_ref, s1_ref)
    mx_bufs = (mx0_ref, mx1_ref)
    chunk = ATTN_ROW_CHUNK
    sub = chunk // SUBLANES

    def fold(x, op):
        return op(x.reshape(sub, SUBLANES, lanes2), axis=0)

    def block_max(par, masked, r0, mx):
        t = s_bufs[par][r0:r0 + chunk, :]
        if masked:
            krow = lax.broadcasted_iota(jnp.int32, (chunk, lanes2), 0) + r0
            qcol = lax.broadcasted_iota(jnp.int32, (chunk, lanes2), 1)
            qcol = jnp.where(qcol >= tq, qcol - tq, qcol)
            t = jnp.where(krow <= qcol, t, -jnp.inf)
            s_bufs[par][r0:r0 + chunk, :] = t
        return jnp.maximum(mx, fold(t, jnp.max))

    def block_exp(shift, par, r0, total):
        p = jnp.exp2(s_bufs[par][r0:r0 + chunk, :] + shift)
        p_ref[r0:r0 + chunk, :] = p.astype(BF16)
        return total + fold(p, jnp.sum)

    def produce_scores(par, k_blk, q2, j):
        cols = slice(j * MXU_TILE, (j + 1) * MXU_TILE)
        t = _dot_nt(k_blk, q2[cols, :]) + rel_ref[:, cols]
        s_bufs[par][:, cols] = t
        mx_bufs[par][:, cols] = jnp.max(t.reshape(tk // SUBLANES, SUBLANES, MXU_TILE), axis=0)

    q2_first = load_q2(0)
    q2_ref[...] = q2_first
    for j in range(lanes2 // MXU_TILE):
        produce_scores(0, k_ref[0:tk, :], q2_first, j)

    def q_block(qi, base):
        q0 = pl.multiple_of(qi * tq, tq)
        q2 = q2_ref[...]
        m_ref[...] = jnp.full(m_ref.shape, -jnp.inf, F32)
        l_ref[...] = jnp.zeros(l_ref.shape, F32)
        acc_ref[...] = jnp.zeros(acc_ref.shape, F32)

        def step(kb, last, par):
            if last:
                q2_next = load_q2(jnp.minimum(qi + 1, nq - 1))
                q2_ref[...] = q2_next
                k_next = k_ref[0:tk, :]
            else:
                q2_next = q2
                k_next = k_ref[pl.ds(pl.multiple_of((kb + 1) * tk, tk), tk), :]

            def scores_piece(j):
                produce_scores(1 - par, k_next, q2_next, j)

            n_piece = lanes2 // MXU_TILE
            n_slab = tk // MXU_TILE
            scores_piece(0)
            if last:
                mx = jnp.full((SUBLANES, lanes2), -jnp.inf, F32)
                for r0 in range(0, tk, chunk):
                    mx = block_max(par, True, r0, mx)
            else:
                mx = mx_bufs[par][...]
            offset = slope * (kb * tk - q0).astype(F32)
            m_prev = m_ref[...]
            m_new = jnp.maximum(m_prev, jnp.max(mx, axis=0, keepdims=True) + offset)
            alpha = jnp.exp2(m_prev - m_new)
            shift = offset - m_new
            total = jnp.zeros((SUBLANES, lanes2), F32)
            pv = None
            for slab in range(n_slab):
                keys = slice(slab * MXU_TILE, (slab + 1) * MXU_TILE)
                for r0 in range(keys.start, keys.stop, chunk):
                    total = block_exp(shift, par, r0, total)
                for j in range(1 + slab * (n_piece - 1) // n_slab, 1 + (slab + 1) * (n_piece - 1) // n_slab):
                    scores_piece(j)
                part = _dot(vt_ref[kb][:, keys], p_ref[keys, :])
                pv = part if pv is None else pv + part
            l_ref[...] = alpha * l_ref[...] + jnp.sum(total, axis=0, keepdims=True)
            acc_ref[...] = alpha * acc_ref[...] + pv
            m_ref[...] = m_new

        def either_parity(kb, last):
            dyn_par = (base + kb) % 2
            for par in range(2):
                pl.when(dyn_par == par)(functools.partial(step, kb, last, par))

        def body(kb, c):
            either_parity(kb, False)
            return c

        lax.fori_loop(0, qi, body, 0)
        either_parity(qi, True)
        o2 = acc_ref[...] / l_ref[...]
        o_t = o2[:, :tq] - lam * o2[:, tq:]
        o_ref[pl.ds(q0, tq), :] = (_rms(o_t.T, sub_gain) * (1.0 - lam_init)).astype(BF16)
        return (base + qi + 1) % 2

    lax.fori_loop(0, nq, q_block, jnp.int32(0))


def _attn(qk, vt, slopes, lam_params, sub_gain, *, bsz, seq, lam_init, tq):
    t = qk.shape[0]
    tk = vt.shape[3]
    assert tk == tq
    kernel = functools.partial(_attn_kernel, tq=tq, tk=tk, lam_init=lam_init)
    return pl.pallas_call(
        kernel,
        out_shape=jax.ShapeDtypeStruct((t, C_HEADS * C_VALUE_DIM), BF16),
        grid_spec=pltpu.PrefetchScalarGridSpec(
            num_scalar_prefetch=1,
            grid=(bsz, C_HEADS),
            in_specs=[pl.BlockSpec((seq, LANES), lambda b, h, s: (b, h)),
                      pl.BlockSpec((seq, LANES), lambda b, h, s: (b, C_HEADS + h)),
                      pl.BlockSpec((None, seq // tk, C_VALUE_DIM, tk), lambda b, h, s: (b, 0, h, 0)),
                      pl.BlockSpec((4, C_HEAD_DIM), lambda b, h, s: (0, 0)),
                      pl.BlockSpec((1, C_VALUE_DIM), lambda b, h, s: (0, 0))],
            out_specs=pl.BlockSpec((seq, LANES), lambda b, h, s: (b, h)),
            scratch_shapes=[pltpu.VMEM((tk, 2 * tq), F32), pltpu.VMEM((tk, 2 * tq), F32),
                            pltpu.VMEM((SUBLANES, 2 * tq), F32), pltpu.VMEM((SUBLANES, 2 * tq), F32),
                            pltpu.VMEM((tk, 2 * tq), BF16),
                            pltpu.VMEM((2 * tq, LANES), BF16),
                            pltpu.VMEM((tk, 2 * tq), F32),
                            pltpu.VMEM((1, 2 * tq), F32), pltpu.VMEM((1, 2 * tq), F32),
                            pltpu.VMEM((C_VALUE_DIM, 2 * tq), F32)]),
        compiler_params=_params("parallel", "parallel"),
        name="diff_attn",
    )(slopes, qk, qk, vt, lam_params, sub_gain)


def _post_kernel(*refs, n_mix, layer, fc):
    h_ref = refs[0]
    mix_refs = refs[1:1 + n_mix]
    (w_out_ref, g_mlp_ref, w1_ref, w2_ref, g_ple_ref, w_gate_ref, w_proj_ref, p_ref, o_ref) = refs[1 + n_mix:]
    del layer
    mix = mix_refs[0][...] if n_mix == 1 else jnp.concatenate([m[...] for m in mix_refs], axis=1)
    h = h_ref[...] + _dot(mix, w_out_ref[...])
    hn = _rms(h, g_mlp_ref[...]).astype(BF16)
    acc = h
    for c in range(0, D_FF, fc):
        a = jnp.maximum(_dot(hn, w1_ref[:, c:c + fc]), 0.0)
        acc = acc + _dot((a * a).astype(BF16), w2_ref[c:c + fc, :])
    h = acc
    gate = jax.nn.sigmoid(_dot(_rms(h, g_ple_ref[...]).astype(BF16), w_gate_ref[...]))
    o_ref[...] = h + _dot(p_ref[...].astype(BF16), w_proj_ref[...]) * gate


def _post(h, mixes, w_out, g_mlp, w1, w2, g_ple, w_gate, w_proj, p_all, layer, *, tm=512, fc=512):
    t = h.shape[0]
    row = lambda i: (i, 0)
    mix_specs = [pl.BlockSpec((tm, m.shape[1]), row) for m in mixes]
    return pl.pallas_call(
        functools.partial(_post_kernel, n_mix=len(mixes), layer=layer, fc=fc),
        out_shape=jax.ShapeDtypeStruct((t, D_MODEL), F32),
        grid=(t // tm,),
        in_specs=[pl.BlockSpec((tm, D_MODEL), row)] + mix_specs + [
            _resident((D_MODEL, D_MODEL)), _resident((1, D_MODEL)),
            _resident((D_MODEL, D_FF)), _resident((D_FF, D_MODEL)),
            _resident((1, D_MODEL)), _resident((D_MODEL, D_MODEL)), _resident((PLE_DIM, D_MODEL)),
            pl.BlockSpec((None, tm, PLE_DIM), lambda i: (layer, i, 0))],
        out_specs=pl.BlockSpec((tm, D_MODEL), row),
        compiler_params=_params("parallel"),
        name="post",
    )(h, *mixes, w_out, g_mlp, w1, w2, g_ple, w_gate, w_proj, p_all)


def _lane_row(values, offset):
    return jnp.zeros((1, LANES), F32).at[0, offset:offset + values.shape[0]].set(values.astype(F32))


def kernel(x, p, ln_mix_e, w_in_e, gmlp_v_gain, gmlp_ws, gmlp_bs, gdn_conv, gdn_a_log, gdn_dt_bias,
           gdn_out_gain, w_out_e, ln_mix_o, w_qkv_o, attn_q_gain, attn_k_gain, diff_lambda,
           attn_sub_gain, w_out_o, ln_mlp, w_mlp1, w_mlp2, ln_ple, w_ple_gate, w_ple_proj):
    bsz, seq, d = x.shape
    depth = p.shape[0]
    t = bsz * seq
    h = x.reshape(t, d)
    p_all = p.reshape(depth, t, PLE_DIM)
    slopes = 2.0 ** (-8.0 * jnp.arange(1, C_HEADS + 1, dtype=F32) / C_HEADS)
    row = lambda a: a.reshape(1, -1).astype(F32)
    for layer in range(depth):
        i = layer // 2
        if layer % 2 == 0:
            w_in = jnp.pad(w_in_e[i], ((0, 0), (0, EVEN_PAD - w_in_e.shape[2]))).astype(BF16)
            proj = _in_even(h, row(ln_mix_e[i]), w_in)
            a_out = _gmlp(proj, row(gmlp_v_gain[i]), gmlp_ws[i], gmlp_bs[i].T)
            b_out = _gdn(proj, gdn_conv[i], _lane_row(gdn_a_log[i], B_HEADS),
                         _lane_row(gdn_dt_bias[i], B_HEADS), row(gdn_out_gain[i]), bsz=bsz, seq=seq)
            mixes = (a_out, b_out)
            w_out = w_out_e[i]
        else:
            lam_init = 0.8 - 0.6 * math.exp(-0.3 * layer)
            qg = row(jnp.concatenate([attn_q_gain[i], attn_q_gain[i]]))
            kg = row(jnp.concatenate([attn_k_gain[i], attn_k_gain[i]]))
            qk, vt = _in_odd(h, row(ln_mix_o[i]), w_qkv_o[i].astype(BF16), qg, kg, bsz=bsz, seq=seq,
                             tk=ATTN_BLOCK)
            o = _attn(qk, vt, slopes, diff_lambda[i], row(attn_sub_gain[i]), bsz=bsz, seq=seq,
                      lam_init=lam_init, tq=ATTN_BLOCK)
            mixes = (o,)
            w_out = w_out_o[i]
        h = _post(h, mixes, w_out.astype(BF16), row(ln_mlp[layer]), w_mlp1[layer].astype(BF16),
                  w_mlp2[layer].astype(BF16), row(ln_ple[layer]), w_ple_gate[layer].astype(BF16),
                  w_ple_proj[layer].astype(BF16), p_all, layer)
    return h.reshape(bsz, seq, d)
```

```python
import functools
import math

import jax
import jax.numpy as jnp
from jax import lax
from jax.experimental import pallas as pl
from jax.experimental.pallas import tpu as pltpu

F32 = jnp.float32
BF16 = jnp.bfloat16
NORM_EPS = 1e-6

LANES = 128
SUBLANES = 8
MXU_TILE = 256
D_MODEL = 1024
PLE_DIM = 256
D_FF = 4 * D_MODEL
A_WIDTH = 512
A_GROUPS = 4
A_CHUNK = 128
B_HEADS = 4
B_HEAD_DIM = 128
B_WIDTH = 512
B_CONV = 4
B_CHUNK = 64
CONV_PAD = 8
C_HEADS = 8
C_HEAD_DIM = 64
C_VALUE_DIM = 128
ATTN_BLOCK = 512
ATTN_ROW_CHUNK = 32
LOG2_E = math.log2(math.e)
EVEN_MAIN = 2 * A_WIDTH + 4 * B_WIDTH
EVEN_PAD = EVEN_MAIN + LANES
VMEM_LIMIT = 56 * 1024 * 1024


def _rms(x, gain):
    return x * lax.rsqrt(jnp.mean(x * x, axis=-1, keepdims=True) + NORM_EPS) * gain


def _dot(a, b):
    return jnp.dot(a, b, preferred_element_type=F32)


def _dot_nt(a, b):
    return lax.dot_general(a, b, (((1,), (1,)), ((), ())), preferred_element_type=F32)


def _resident(shape):
    zeros = (0,) * len(shape)
    return pl.BlockSpec(shape, lambda *_: zeros, pipeline_mode=pl.Buffered(1))


def _stacked(shape, index):
    zeros = (0,) * len(shape)
    return pl.BlockSpec((None,) + shape, lambda *_: (index,) + zeros, pipeline_mode=pl.Buffered(1))


def _params(*sem):
    return pltpu.CompilerParams(dimension_semantics=sem, vmem_limit_bytes=VMEM_LIMIT)


def _in_even_kernel(h_ref, g_ref, w_ref, vg_ref, ws_ref, bs_ref, a_ref, o_ref, *, tm, tn):
    hn = _rms(h_ref[...], g_ref[...]).astype(BF16)
    u = jax.nn.gelu(_dot(hn, w_ref[:, 0:A_WIDTH]))
    v = jax.nn.gelu(_dot(hn, w_ref[:, A_WIDTH:2 * A_WIDTH]))
    for start in range(2 * A_WIDTH, EVEN_PAD, tn):
        width = min(tn, EVEN_PAD - start)
        out0 = start - 2 * A_WIDTH
        o_ref[:, out0:out0 + width] = _dot(hn, w_ref[:, start:start + width])
    ii = lax.broadcasted_iota(jnp.int32, (A_CHUNK, A_CHUNK), 0)
    jj = lax.broadcasted_iota(jnp.int32, (A_CHUNK, A_CHUNK), 1)
    for g in range(A_GROUPS):
        cols = slice(g * LANES, (g + 1) * LANES)
        w = jnp.where(ii >= jj, ws_ref[g], 0.0).astype(BF16)
        bias = bs_ref[:, g:g + 1]
        gain = vg_ref[:, cols]
        for c in range(tm // A_CHUNK):
            rows = slice(c * A_CHUNK, (c + 1) * A_CHUNK)
            vg = v[rows, cols]
            vc = vg - jnp.mean(vg, axis=-1, keepdims=True)
            y = vc * lax.rsqrt(jnp.mean(vc * vc, axis=-1, keepdims=True) + NORM_EPS) * gain
            s = _dot(w, y.astype(BF16)) + bias
            a_ref[rows, cols] = (u[rows, cols] * s).astype(BF16)


def _in_even(h, gain, w_all, v_gain, ws_all, bs_t, i, *, tm=512, tn=512):
    t = h.shape[0]
    return pl.pallas_call(
        functools.partial(_in_even_kernel, tm=tm, tn=tn),
        out_shape=(jax.ShapeDtypeStruct((t, A_WIDTH), BF16),
                   jax.ShapeDtypeStruct((t, EVEN_PAD - 2 * A_WIDTH), F32)),
        grid=(t // tm,),
        in_specs=[pl.BlockSpec((tm, D_MODEL), lambda i_: (i_, 0)),
                  _resident((1, D_MODEL)), _stacked((D_MODEL, EVEN_PAD), i),
                  _resident((1, A_WIDTH)), _stacked((A_GROUPS, A_CHUNK, A_CHUNK), i),
                  _resident((A_CHUNK, A_GROUPS))],
        out_specs=(pl.BlockSpec((tm, A_WIDTH), lambda i_: (i_, 0)),
                   pl.BlockSpec((tm, EVEN_PAD - 2 * A_WIDTH), lambda i_: (i_, 0))),
        compiler_params=_params("parallel"),
        name="in_even",
    )(h, gain, w_all, v_gain, ws_all, bs_t)


def _unit_lower_inverses(a_list, eye, level_masks):
    ds = [eye - jnp.where(level_masks[0], a, 0.0) for a in a_list]
    for mask in level_masks[1:]:
        d16 = [d.astype(BF16) for d in ds]
        mds = [_dot(jnp.where(mask, a, 0.0).astype(BF16), d) for a, d in zip(a_list, d16)]
        ds = [d - _dot(d_lo, md.astype(BF16)) for d, d_lo, md in zip(ds, d16, mds)]
    return ds


def _gdn_kernel(q_ref, k_ref, v_ref, z_ref, ba_ref, cwq_ref, cwk_ref, cwv_ref, alog_ref, dtb_ref,
                og_ref, o_ref, buf_ref, qn_ref, kn_ref, vn_ref, state_ref, lhs_ref, ds_ref, o0_ref,
                gl_ref, raw_ref, *, tb, group):
    c_len = B_CHUNK

    @pl.when(pl.program_id(1) == 0)
    def _():
        buf_ref[:, 0:CONV_PAD, :] = jnp.zeros((3, CONV_PAD, B_WIDTH), F32)
        state_ref[...] = jnp.zeros_like(state_ref)

    for idx, (x_ref, cw_ref, dst_ref) in enumerate(
            ((q_ref, cwq_ref, qn_ref), (k_ref, cwk_ref, kn_ref), (v_ref, cwv_ref, vn_ref))):
        x = x_ref[...]
        buf_ref[idx, CONV_PAD:CONV_PAD + tb, :] = x
        cw = cw_ref[...]
        y = cw[B_CONV - 1:B_CONV, :] * x
        for j in range(B_CONV - 1):
            off = CONV_PAD - (B_CONV - 1) + j
            y = y + cw[j:j + 1, :] * buf_ref[idx, off:off + tb, :]
        buf_ref[idx, 0:CONV_PAD, :] = x[tb - CONV_PAD:tb, :]
        y = y * jax.nn.sigmoid(y)
        if idx == 2:
            dst_ref[...] = y
        else:
            scale = B_HEAD_DIM ** -0.5 if idx == 0 else 1.0
            for h in range(B_HEADS):
                cols = slice(h * LANES, (h + 1) * LANES)
                yh = y[:, cols]
                yh = yh * lax.rsqrt(jnp.sum(yh * yh, axis=-1, keepdims=True) + NORM_EPS)
                dst_ref[:, cols] = yh * scale if idx == 0 else yh

    ii = lax.broadcasted_iota(jnp.int32, (c_len, c_len), 0)
    jj = lax.broadcasted_iota(jnp.int32, (c_len, c_len), 1)
    causal = ii >= jj
    strict = ii > jj
    eye = jnp.where(ii == jj, 1.0, 0.0).astype(F32)
    ltri = jnp.where(causal, 1.0, 0.0).astype(BF16)
    level_masks = []
    s = 1
    while s < c_len:
        level_masks.append(((ii // (2 * s)) == (jj // (2 * s))) & ((ii % (2 * s)) >= s) & ((jj % (2 * s)) < s))
        s *= 2
    neg_rate = -jnp.exp(alog_ref[...])
    dt_bias = dtb_ref[...]
    out_gain = og_ref[...]

    heads = range(B_HEADS)

    def prepare(gi, carry):
        prob = []
        for cc in range(group):
            c = gi * group + cc
            rows = pl.ds(pl.multiple_of(c * c_len, c_len), c_len)
            ba = ba_ref[rows, :]
            beta = jax.nn.sigmoid(ba)
            g = neg_rate * jax.nn.softplus(ba + dt_bias)
            g_hi = g.astype(BF16)
            g_lo = (g - g_hi.astype(F32)).astype(BF16)
            gc = _dot(ltri, g_hi) + _dot(ltri, g_lo)
            gc_t = gc.T
            for h in heads:
                cols = slice(h * LANES, (h + 1) * LANES)
                q_h = qn_ref[rows, cols]
                k_h = kn_ref[rows, cols]
                gcol = gc[:, B_HEADS + h:B_HEADS + h + 1]
                grow = gc_t[B_HEADS + h:B_HEADS + h + 1, :]
                glast = gc[c_len - 1:c_len, B_HEADS + h:B_HEADS + h + 1]
                bcol = beta[:, h:h + 1]
                eg = jnp.exp(gcol)
                k_beta = k_h * bcol
                prob.append(dict(
                    c=c, h=h, q_dec=q_h * eg,
                    decay=jnp.where(causal, jnp.exp(jnp.where(causal, gcol - grow, 0.0)), 0.0),
                    lhs=jnp.concatenate([q_h, k_beta], axis=0).astype(BF16), k16=k_h.astype(BF16),
                    rhs=jnp.concatenate([k_beta * eg, vn_ref[rows, cols] * bcol], axis=1).astype(BF16),
                    k_dec=k_h * jnp.exp(glast - gcol), gl=jnp.exp(glast)))
        kks = [_dot_nt(p["lhs"], p["k16"]) for p in prob]
        a_list = [jnp.where(strict, kk[c_len:] * p["decay"], 0.0) for kk, p in zip(kks, prob)]
        t_invs = _unit_lower_inverses(a_list, eye, level_masks)
        wus = [_dot(t.astype(BF16), p["rhs"]).astype(BF16) for t, p in zip(t_invs, prob)]
        kd_wu = [_dot(p["k_dec"].T.astype(BF16), wu) for p, wu in zip(prob, wus)]
        qk_wu = [_dot((kk[:c_len] * p["decay"]).astype(BF16), wu) for p, kk, wu in zip(prob, kks, wus)]
        for p, kd, qk in zip(prob, kd_wu, qk_wu):
            c, h = p["c"], p["h"]
            lhs_ref[c, h] = jnp.concatenate(
                [kd[:, :B_HEAD_DIM], qk[:, :B_HEAD_DIM] - p["q_dec"]], axis=0).astype(BF16)
            ds_ref[c, h] = kd[:, B_HEAD_DIM:]
            o0_ref[c, h] = qk[:, B_HEAD_DIM:]
            gl_ref[c, h] = jnp.broadcast_to(p["gl"], (1, LANES))
        return carry

    lax.fori_loop(0, tb // (c_len * group), prepare, 0)

    def recur(c, carry):
        rows = pl.ds(pl.multiple_of(c * c_len, c_len), c_len)
        states = [state_ref[h] for h in heads]
        prods = [_dot(lhs_ref[c, h], states[h].astype(BF16)) for h in heads]
        for h in heads:
            state_ref[h] = states[h] * gl_ref[c, h] + (ds_ref[c, h] - prods[h][:B_HEAD_DIM])
            raw_ref[rows, h * LANES:(h + 1) * LANES] = o0_ref[c, h] - prods[h][B_HEAD_DIM:]
        return carry

    lax.fori_loop(0, tb // c_len, recur, 0)

    for h in heads:
        cols = slice(h * LANES, (h + 1) * LANES)
        z = z_ref[:, cols]
        o_ref[:, cols] = (_rms(raw_ref[:, cols], out_gain) * (z * jax.nn.sigmoid(z))).astype(BF16)


def _gdn(proj, conv_all, alog_row, dtb_row, out_gain, i_layer, *, bsz, seq, tb=512, group=8):
    t = proj.shape[0]
    nt = seq // tb

    def col(j):
        return pl.BlockSpec((tb, B_WIDTH), lambda b, i: (b * nt + i, j))

    def conv(j):
        return pl.BlockSpec((None, B_CONV, B_WIDTH), lambda b, i: (i_layer, 0, j))

    nc = tb // B_CHUNK
    per_problem = lambda *shape_dtype: pltpu.VMEM((nc, B_HEADS) + shape_dtype[:-1], shape_dtype[-1])
    return pl.pallas_call(
        functools.partial(_gdn_kernel, tb=tb, group=group),
        out_shape=jax.ShapeDtypeStruct((t, B_WIDTH), BF16),
        grid=(bsz, nt),
        in_specs=[col(0), col(1), col(2), col(3),
                  pl.BlockSpec((tb, LANES), lambda b, i: (b * nt + i, 4 * B_WIDTH // LANES)),
                  conv(0), conv(1), conv(2),
                  pl.BlockSpec((1, LANES), lambda b, i: (0, 0)),
                  pl.BlockSpec((1, LANES), lambda b, i: (0, 0)),
                  pl.BlockSpec((1, B_HEAD_DIM), lambda b, i: (0, 0))],
        out_specs=pl.BlockSpec((tb, B_WIDTH), lambda b, i: (b * nt + i, 0)),
        scratch_shapes=[pltpu.VMEM((3, CONV_PAD + tb, B_WIDTH), F32),
                        pltpu.VMEM((tb, B_WIDTH), F32), pltpu.VMEM((tb, B_WIDTH), F32),
                        pltpu.VMEM((tb, B_WIDTH), F32),
                        pltpu.VMEM((B_HEADS, B_HEAD_DIM, B_HEAD_DIM), F32),
                        per_problem(B_HEAD_DIM + B_CHUNK, B_HEAD_DIM, BF16),
                        per_problem(B_HEAD_DIM, B_HEAD_DIM, F32),
                        per_problem(B_CHUNK, B_HEAD_DIM, F32),
                        per_problem(1, LANES, F32),
                        pltpu.VMEM((tb, B_WIDTH), F32)],
        compiler_params=_params("parallel", "arbitrary"),
        name="gdn",
    )(proj, proj, proj, proj, proj, conv_all, conv_all, conv_all, alog_row, dtb_row, out_gain)


def _in_odd_kernel(h_ref, g_ref, w_ref, qg_ref, kg_ref, o_ref, vt_ref, *, tm, tn, tk):
    hn = _rms(h_ref[...], g_ref[...]).astype(BF16)
    lane = lax.broadcasted_iota(jnp.int32, (tm, LANES), 1)
    low = lane < C_HEAD_DIM
    qk_width = 2 * C_HEADS * 2 * C_HEAD_DIM
    for start in range(0, 3 * C_HEADS * C_VALUE_DIM, tn):
        y = _dot(hn, w_ref[:, start:start + tn])
        if start >= qk_width:
            for r0 in range(0, tm, MXU_TILE):
                vt_ref[r0 // tk, start - qk_width:start - qk_width + tn, r0 % tk:r0 % tk + MXU_TILE] = (
                    y[r0:r0 + MXU_TILE, :].T.astype(BF16))
            continue
        is_q = start < qk_width // 2
        gain = qg_ref[...] if is_q else kg_ref[...]
        for j in range(tn // LANES):
            yb = y[:, j * LANES:(j + 1) * LANES]
            y2 = yb * yb
            lo = jnp.sum(jnp.where(low, y2, 0.0), axis=-1, keepdims=True)
            hi = jnp.sum(jnp.where(low, 0.0, y2), axis=-1, keepdims=True)
            ms = jnp.where(low, lo, hi) * (1.0 / C_HEAD_DIM)
            yn = yb * lax.rsqrt(ms + NORM_EPS) * gain
            if is_q:
                yn = yn * (C_HEAD_DIM ** -0.5 * LOG2_E)
            o_ref[:, start + j * LANES:start + (j + 1) * LANES] = yn.astype(BF16)


def _in_odd(h, gain, w_all, qg, kg, i_layer, *, bsz, seq, tk, tm=512, tn=512):
    t = h.shape[0]
    n = w_all.shape[2]
    nt = seq // tm
    qk_width = 2 * C_HEADS * 2 * C_HEAD_DIM
    return pl.pallas_call(
        functools.partial(_in_odd_kernel, tm=tm, tn=tn, tk=tk),
        out_shape=(jax.ShapeDtypeStruct((t, qk_width), BF16),
                   jax.ShapeDtypeStruct((bsz, seq // tk, n - qk_width, tk), BF16)),
        grid=(t // tm,),
        in_specs=[pl.BlockSpec((tm, D_MODEL), lambda i: (i, 0)),
                  _resident((1, D_MODEL)), _stacked((D_MODEL, n), i_layer),
                  _resident((1, LANES)), _resident((1, LANES))],
        out_specs=(pl.BlockSpec((tm, qk_width), lambda i: (i, 0)),
                   pl.BlockSpec((None, tm // tk, n - qk_width, tk), lambda i: (i // nt, i % nt, 0, 0))),
        compiler_params=_params("parallel"),
        name="in_odd",
    )(h, gain, w_all, qg, kg)


def _attn_kernel(slopes_ref, q_ref, k_ref, vt_ref, lam_ref, sg_ref, o_ref, s0_ref, s1_ref, mx0_ref,
                 mx1_ref, p_ref, q2_ref, rel_ref, m_ref, l_ref, acc_ref, *, tq, tk, lam_init):
    slope = slopes_ref[pl.program_id(1)] * LOG2_E
    nq = q_ref.shape[0] // tq
    ki = lax.broadcasted_iota(jnp.int32, (tk, 2 * tq), 0)
    qj = lax.broadcasted_iota(jnp.int32, (tk, 2 * tq), 1)
    qj = jnp.where(qj >= tq, qj - tq, qj)
    rel_ref[...] = (ki - qj).astype(F32) * slope
    lane = lax.broadcasted_iota(jnp.int32, (tq, LANES), 1)
    lp = lam_ref[...]
    lam = (jnp.exp(jnp.sum(lp[0:1] * lp[1:2], axis=-1, keepdims=True))
           - jnp.exp(jnp.sum(lp[2:3] * lp[3:4], axis=-1, keepdims=True)) + lam_init)
    sub_gain = sg_ref[...]

    def load_q2(qi):
        q = q_ref[pl.ds(pl.multiple_of(qi * tq, tq), tq), :]
        zero = jnp.zeros_like(q)
        return jnp.concatenate([jnp.where(lane < C_HEAD_DIM, q, zero),
                                jnp.where(lane < C_HEAD_DIM, zero, q)], axis=0)

    lanes2 = 2 * tq
    s_bufs = (s0_ref, s1_ref)
    mx_bufs = (mx0_ref, mx1_ref)
    chunk = ATTN_ROW_CHUNK
    sub = chunk // SUBLANES

    def fold(x, op):
        return op(x.reshape(sub, SUBLANES, lanes2), axis=0)

    def masked_max(par, r0, mx):
        krow = lax.broadcasted_iota(jnp.int32, (chunk, lanes2), 0) + r0
        qcol = lax.broadcasted_iota(jnp.int32, (chunk, lanes2), 1)
        qcol = jnp.where(qcol >= tq, qcol - tq, qcol)
        t = jnp.where(krow <= qcol, s_bufs[par][r0:r0 + chunk, :], -jnp.inf)
        s_bufs[par][r0:r0 + chunk, :] = t
        return jnp.maximum(mx, fold(t, jnp.max))

    def block_exp(shift, par, r0, total):
        p = jnp.exp2(s_bufs[par][r0:r0 + chunk, :] + shift)
        p_ref[r0:r0 + chunk, :] = p.astype(BF16)
        return total + fold(p, jnp.sum)

    def produce_scores(par, k_blk, q2, j):
        cols = slice(j * MXU_TILE, (j + 1) * MXU_TILE)
        t = _dot_nt(k_blk, q2[cols, :]) + rel_ref[:, cols]
        s_bufs[par][:, cols] = t
        mx_bufs[par][:, cols] = jnp.max(t.reshape(tk // SUBLANES, SUBLANES, MXU_TILE), axis=0)

    q2_first = load_q2(0)
    q2_ref[...] = q2_first
    for j in range(lanes2 // MXU_TILE):
        produce_scores(0, k_ref[0:tk, :], q2_first, j)

    def q_block(qi, base):
        q0 = pl.multiple_of(qi * tq, tq)
        q2 = q2_ref[...]
        m_ref[...] = jnp.full(m_ref.shape, -jnp.inf, F32)
        l_ref[...] = jnp.zeros(l_ref.shape, F32)
        acc_ref[...] = jnp.zeros(acc_ref.shape, F32)

        def step(kb, last, par):
            if last:
                q2_next = load_q2(jnp.minimum(qi + 1, nq - 1))
                q2_ref[...] = q2_next
                k_next = k_ref[0:tk, :]
            else:
                q2_next = q2
                k_next = k_ref[pl.ds(pl.multiple_of((kb + 1) * tk, tk), tk), :]

            def scores_piece(j):
                produce_scores(1 - par, k_next, q2_next, j)

            n_piece = lanes2 // MXU_TILE
            n_slab = tk // MXU_TILE
            scores_piece(0)
            if last:
                mx = jnp.full((SUBLANES, lanes2), -jnp.inf, F32)
                for r0 in range(0, tk, chunk):
                    mx = masked_max(par, r0, mx)
            else:
                mx = mx_bufs[par][...]
            offset = slope * (kb * tk - q0).astype(F32)
            m_prev = m_ref[...]
            m_new = jnp.maximum(m_prev, jnp.max(mx, axis=0, keepdims=True) + offset)
            alpha = jnp.exp2(m_prev - m_new)
            shift = offset - m_new
            total = jnp.zeros((SUBLANES, lanes2), F32)
            pv = None
            for slab in range(n_slab):
                keys = slice(slab * MXU_TILE, (slab + 1) * MXU_TILE)
                for r0 in range(keys.start, keys.stop, chunk):
                    total = block_exp(shift, par, r0, total)
                first = 1 + slab * (n_piece - 1) // n_slab
                for j in range(first, 1 + (slab + 1) * (n_piece - 1) // n_slab):
                    scores_piece(j)
                part = _dot(vt_ref[kb][:, keys], p_ref[keys, :])
                pv = part if pv is None else pv + part
            l_ref[...] = alpha * l_ref[...] + jnp.sum(total, axis=0, keepdims=True)
            acc_ref[...] = alpha * acc_ref[...] + pv
            m_ref[...] = m_new

        def either_parity(kb, last):
            dyn_par = (base + kb) % 2
            for par in range(2):
                pl.when(dyn_par == par)(functools.partial(step, kb, last, par))

        def body(kb, c):
            either_parity(kb, False)
            return c

        lax.fori_loop(0, qi, body, 0)
        either_parity(qi, True)
        o2 = acc_ref[...] / l_ref[...]
        o_t = o2[:, :tq] - lam * o2[:, tq:]
        o_t = o_t * lax.rsqrt(jnp.mean(o_t * o_t, axis=0, keepdims=True) + NORM_EPS)
        o_ref[pl.ds(q0, tq), :] = (o_t.T * sub_gain * (1.0 - lam_init)).astype(BF16)
        return (base + qi + 1) % 2

    lax.fori_loop(0, nq, q_block, jnp.int32(0))


def _attn(qk, vt, slopes, lam_params, sub_gain, *, bsz, seq, lam_init, tq):
    t = qk.shape[0]
    tk = vt.shape[3]
    assert tk == tq
    kernel = functools.partial(_attn_kernel, tq=tq, tk=tk, lam_init=lam_init)
    return pl.pallas_call(
        kernel,
        out_shape=jax.ShapeDtypeStruct((t, C_HEADS * C_VALUE_DIM), BF16),
        grid_spec=pltpu.PrefetchScalarGridSpec(
            num_scalar_prefetch=1,
            grid=(bsz, C_HEADS),
            in_specs=[pl.BlockSpec((seq, LANES), lambda b, h, s: (b, h)),
                      pl.BlockSpec((seq, LANES), lambda b, h, s: (b, C_HEADS + h)),
                      pl.BlockSpec((None, seq // tk, C_VALUE_DIM, tk), lambda b, h, s: (b, 0, h, 0)),
                      pl.BlockSpec((4, C_HEAD_DIM), lambda b, h, s: (0, 0)),
                      pl.BlockSpec((1, C_VALUE_DIM), lambda b, h, s: (0, 0))],
            out_specs=pl.BlockSpec((seq, LANES), lambda b, h, s: (b, h)),
            scratch_shapes=[pltpu.VMEM((tk, 2 * tq), F32), pltpu.VMEM((tk, 2 * tq), F32),
                            pltpu.VMEM((SUBLANES, 2 * tq), F32), pltpu.VMEM((SUBLANES, 2 * tq), F32),
                            pltpu.VMEM((tk, 2 * tq), BF16),
                            pltpu.VMEM((2 * tq, LANES), BF16),
                            pltpu.VMEM((tk, 2 * tq), F32),
                            pltpu.VMEM((1, 2 * tq), F32), pltpu.VMEM((1, 2 * tq), F32),
                            pltpu.VMEM((C_VALUE_DIM, 2 * tq), F32)]),
        compiler_params=_params("parallel", "parallel"),
        name="diff_attn",
    )(slopes, qk, qk, vt, lam_params, sub_gain)


def _post_kernel(*refs, n_mix, layer, fc):
    h_ref = refs[0]
    mix_refs = refs[1:1 + n_mix]
    (w_out_ref, g_mlp_ref, w1_ref, w2_ref, g_ple_ref, w_gate_ref, w_proj_ref, p_ref, o_ref) = refs[1 + n_mix:]
    del layer
    mix = mix_refs[0][...] if n_mix == 1 else jnp.concatenate([m[...] for m in mix_refs], axis=1)
    h = h_ref[...] + _dot(mix, w_out_ref[...])
    hn = _rms(h, g_mlp_ref[...]).astype(BF16)
    acc = h
    for c in range(0, D_FF, fc):
        a = jnp.maximum(_dot(hn, w1_ref[:, c:c + fc]), 0.0)
        acc = acc + _dot((a * a).astype(BF16), w2_ref[c:c + fc, :])
    h = acc
    gate = jax.nn.sigmoid(_dot(_rms(h, g_ple_ref[...]).astype(BF16), w_gate_ref[...]))
    o_ref[...] = h + _dot(p_ref[...].astype(BF16), w_proj_ref[...]) * gate


def _post(h, mixes, w_out_all, i_mix, g_mlp, w1_all, w2_all, g_ple, w_gate_all, w_proj_all, p_all, layer,
          *, tm=512, fc=512):
    t = h.shape[0]
    row = lambda i: (i, 0)
    mix_specs = [pl.BlockSpec((tm, m.shape[1]), row) for m in mixes]
    return pl.pallas_call(
        functools.partial(_post_kernel, n_mix=len(mixes), layer=layer, fc=fc),
        out_shape=jax.ShapeDtypeStruct((t, D_MODEL), F32),
        grid=(t // tm,),
        in_specs=[pl.BlockSpec((tm, D_MODEL), row)] + mix_specs + [
            _stacked((D_MODEL, D_MODEL), i_mix), _resident((1, D_MODEL)),
            _stacked((D_MODEL, D_FF), layer), _stacked((D_FF, D_MODEL), layer),
            _resident((1, D_MODEL)), _stacked((D_MODEL, D_MODEL), layer),
            _stacked((PLE_DIM, D_MODEL), layer),
            pl.BlockSpec((None, tm, PLE_DIM), lambda i: (layer, i, 0))],
        out_specs=pl.BlockSpec((tm, D_MODEL), row),
        compiler_params=_params("parallel"),
        name="post",
    )(h, *mixes, w_out_all, g_mlp, w1_all, w2_all, g_ple, w_gate_all, w_proj_all, p_all)


def _lane_row(values, offset):
    return jnp.zeros((1, LANES), F32).at[0, offset:offset + values.shape[0]].set(values.astype(F32))


def kernel(x, p, ln_mix_e, w_in_e, gmlp_v_gain, gmlp_ws, gmlp_bs, gdn_conv, gdn_a_log, gdn_dt_bias,
           gdn_out_gain, w_out_e, ln_mix_o, w_qkv_o, attn_q_gain, attn_k_gain, diff_lambda,
           attn_sub_gain, w_out_o, ln_mlp, w_mlp1, w_mlp2, ln_ple, w_ple_gate, w_ple_proj):
    bsz, seq, d = x.shape
    depth = p.shape[0]
    t = bsz * seq
    h = x.reshape(t, d)
    p_all = p.reshape(depth, t, PLE_DIM)
    slopes = 2.0 ** (-8.0 * jnp.arange(1, C_HEADS + 1, dtype=F32) / C_HEADS)
    row = lambda a: a.reshape(1, -1).astype(F32)
    w_in_e16 = jnp.pad(w_in_e, ((0, 0), (0, 0), (0, EVEN_PAD - w_in_e.shape[2]))).astype(BF16)
    w_qkv_o16 = w_qkv_o.astype(BF16)
    w_out_e16, w_out_o16 = w_out_e.astype(BF16), w_out_o.astype(BF16)
    w_mlp1_16, w_mlp2_16 = w_mlp1.astype(BF16), w_mlp2.astype(BF16)
    w_gate16, w_proj16 = w_ple_gate.astype(BF16), w_ple_proj.astype(BF16)
    for layer in range(depth):
        i = layer // 2
        if layer % 2 == 0:
            a_out, proj = _in_even(h, row(ln_mix_e[i]), w_in_e16, row(gmlp_v_gain[i]), gmlp_ws,
                                   gmlp_bs[i].T, i)
            b_out = _gdn(proj, gdn_conv, _lane_row(gdn_a_log[i], B_HEADS),
                         _lane_row(gdn_dt_bias[i], B_HEADS), row(gdn_out_gain[i]), i, bsz=bsz, seq=seq)
            mixes = (a_out, b_out)
            w_out = w_out_e16
        else:
            lam_init = 0.8 - 0.6 * math.exp(-0.3 * layer)
            qg = row(jnp.concatenate([attn_q_gain[i], attn_q_gain[i]]))
            kg = row(jnp.concatenate([attn_k_gain[i], attn_k_gain[i]]))
            qk, vt = _in_odd(h, row(ln_mix_o[i]), w_qkv_o16, qg, kg, i, bsz=bsz, seq=seq, tk=ATTN_BLOCK)
            o = _attn(qk, vt, slopes, diff_lambda[i], row(attn_sub_gain[i]), bsz=bsz, seq=seq,
                      lam_init=lam_init, tq=ATTN_BLOCK)
            mixes = (o,)
            w_out = w_out_o16
        h = _post(h, mixes, w_out, i, row(ln_mlp[layer]), w_mlp1_16, w_mlp2_16, row(ln_ple[layer]),
                  w_gate16, w_proj16, p_all, layer)
    return h.reshape(bsz, seq, d)
```

```python
import functools
import math

import jax
import jax.numpy as jnp
from jax import lax
from jax.experimental import pallas as pl
from jax.experimental.pallas import tpu as pltpu

F32 = jnp.float32
BF16 = jnp.bfloat16
NORM_EPS = 1e-6

LANES = 128
SUBLANES = 8
MXU_TILE = 256
D_MODEL = 1024
PLE_DIM = 256
D_FF = 4 * D_MODEL
A_WIDTH = 512
A_GROUPS = 4
A_CHUNK = 128
B_HEADS = 4
B_HEAD_DIM = 128
B_WIDTH = 512
B_CONV = 4
B_CHUNK = 64
CONV_PAD = 8
C_HEADS = 8
C_HEAD_DIM = 64
C_VALUE_DIM = 128
ATTN_BLOCK = 512
ATTN_ROW_CHUNK = 32
V_ROWS = C_VALUE_DIM + 16
LOG2_E = math.log2(math.e)
EVEN_MAIN = 2 * A_WIDTH + 4 * B_WIDTH
EVEN_PAD = EVEN_MAIN + LANES
VMEM_LIMIT = 56 * 1024 * 1024


def _rms(x, gain):
    return x * lax.rsqrt(jnp.mean(x * x, axis=-1, keepdims=True) + NORM_EPS) * gain


def _dot(a, b):
    return jnp.dot(a, b, preferred_element_type=F32)


def _dot_nt(a, b):
    return lax.dot_general(a, b, (((1,), (1,)), ((), ())), preferred_element_type=F32)


def _resident(shape):
    zeros = (0,) * len(shape)
    return pl.BlockSpec(shape, lambda *_: zeros, pipeline_mode=pl.Buffered(1))


def _stacked(shape, index):
    zeros = (0,) * len(shape)
    return pl.BlockSpec((None,) + shape, lambda *_: (index,) + zeros, pipeline_mode=pl.Buffered(1))


def _params(*sem):
    return pltpu.CompilerParams(dimension_semantics=sem, vmem_limit_bytes=VMEM_LIMIT)


def _in_even_kernel(h_ref, g_ref, w_ref, vg_ref, ws_ref, bs_ref, a_ref, o_ref, *, tm, tn):
    hn = _rms(h_ref[...], g_ref[...]).astype(BF16)
    u = jax.nn.gelu(_dot(hn, w_ref[:, 0:A_WIDTH]))
    v = jax.nn.gelu(_dot(hn, w_ref[:, A_WIDTH:2 * A_WIDTH]))
    for start in range(2 * A_WIDTH, EVEN_PAD, tn):
        width = min(tn, EVEN_PAD - start)
        out0 = start - 2 * A_WIDTH
        o_ref[:, out0:out0 + width] = _dot(hn, w_ref[:, start:start + width])
    ii = lax.broadcasted_iota(jnp.int32, (A_CHUNK, A_CHUNK), 0)
    jj = lax.broadcasted_iota(jnp.int32, (A_CHUNK, A_CHUNK), 1)
    for g in range(A_GROUPS):
        cols = slice(g * LANES, (g + 1) * LANES)
        w = jnp.where(ii >= jj, ws_ref[g], 0.0).astype(BF16)
        bias = bs_ref[:, g:g + 1]
        gain = vg_ref[:, cols]
        for c in range(tm // A_CHUNK):
            rows = slice(c * A_CHUNK, (c + 1) * A_CHUNK)
            vg = v[rows, cols]
            vc = vg - jnp.mean(vg, axis=-1, keepdims=True)
            y = vc * lax.rsqrt(jnp.mean(vc * vc, axis=-1, keepdims=True) + NORM_EPS) * gain
            s = _dot(w, y.astype(BF16)) + bias
            a_ref[rows, cols] = (u[rows, cols] * s).astype(BF16)


def _in_even(h, gain, w_all, v_gain, ws_all, bs_t, i, *, tm=512, tn=512):
    t = h.shape[0]
    return pl.pallas_call(
        functools.partial(_in_even_kernel, tm=tm, tn=tn),
        out_shape=(jax.ShapeDtypeStruct((t, A_WIDTH), BF16),
                   jax.ShapeDtypeStruct((t, EVEN_PAD - 2 * A_WIDTH), F32)),
        grid=(t // tm,),
        in_specs=[pl.BlockSpec((tm, D_MODEL), lambda i_: (i_, 0)),
                  _resident((1, D_MODEL)), _stacked((D_MODEL, EVEN_PAD), i),
                  _resident((1, A_WIDTH)), _stacked((A_GROUPS, A_CHUNK, A_CHUNK), i),
                  _resident((A_CHUNK, A_GROUPS))],
        out_specs=(pl.BlockSpec((tm, A_WIDTH), lambda i_: (i_, 0)),
                   pl.BlockSpec((tm, EVEN_PAD - 2 * A_WIDTH), lambda i_: (i_, 0))),
        compiler_params=_params("parallel"),
        name="in_even",
    )(h, gain, w_all, v_gain, ws_all, bs_t)


def _unit_lower_inverses(a_list, eye, level_masks):
    ds = [eye - jnp.where(level_masks[0], a, 0.0) for a in a_list]
    for mask in level_masks[1:]:
        d16 = [d.astype(BF16) for d in ds]
        mds = [_dot(jnp.where(mask, a, 0.0).astype(BF16), d) for a, d in zip(a_list, d16)]
        ds = [d - _dot(d_lo, md.astype(BF16)) for d, d_lo, md in zip(ds, d16, mds)]
    return ds


def _gdn_kernel(q_ref, k_ref, v_ref, z_ref, ba_ref, cwq_ref, cwk_ref, cwv_ref, alog_ref, dtb_ref,
                og_ref, o_ref, buf_ref, qn_ref, kn_ref, vn_ref, state_ref, lhs_ref, ds_ref, o0_ref,
                gl_ref, raw_ref, *, tb, group):
    c_len = B_CHUNK

    @pl.when(pl.program_id(1) == 0)
    def _():
        buf_ref[:, 0:CONV_PAD, :] = jnp.zeros((3, CONV_PAD, B_WIDTH), F32)
        state_ref[...] = jnp.zeros_like(state_ref)

    for idx, (x_ref, cw_ref, dst_ref) in enumerate(
            ((q_ref, cwq_ref, qn_ref), (k_ref, cwk_ref, kn_ref), (v_ref, cwv_ref, vn_ref))):
        x = x_ref[...]
        buf_ref[idx, CONV_PAD:CONV_PAD + tb, :] = x
        cw = cw_ref[...]
        y = cw[B_CONV - 1:B_CONV, :] * x
        for j in range(B_CONV - 1):
            off = CONV_PAD - (B_CONV - 1) + j
            y = y + cw[j:j + 1, :] * buf_ref[idx, off:off + tb, :]
        buf_ref[idx, 0:CONV_PAD, :] = x[tb - CONV_PAD:tb, :]
        y = y * jax.nn.sigmoid(y)
        if idx == 2:
            dst_ref[...] = y
        else:
            scale = B_HEAD_DIM ** -0.5 if idx == 0 else 1.0
            for h in range(B_HEADS):
                cols = slice(h * LANES, (h + 1) * LANES)
                yh = y[:, cols]
                yh = yh * lax.rsqrt(jnp.sum(yh * yh, axis=-1, keepdims=True) + NORM_EPS)
                dst_ref[:, cols] = yh * scale if idx == 0 else yh

    ii = lax.broadcasted_iota(jnp.int32, (c_len, c_len), 0)
    jj = lax.broadcasted_iota(jnp.int32, (c_len, c_len), 1)
    causal = ii >= jj
    strict = ii > jj
    eye = jnp.where(ii == jj, 1.0, 0.0).astype(F32)
    ltri = jnp.where(causal, 1.0, 0.0).astype(BF16)
    level_masks = []
    s = 1
    while s < c_len:
        level_masks.append(((ii // (2 * s)) == (jj // (2 * s))) & ((ii % (2 * s)) >= s) & ((jj % (2 * s)) < s))
        s *= 2
    neg_rate = -jnp.exp(alog_ref[...])
    dt_bias = dtb_ref[...]
    out_gain = og_ref[...]

    heads = range(B_HEADS)

    def prepare(gi, carry):
        prob = []
        for cc in range(group):
            c = gi * group + cc
            rows = pl.ds(pl.multiple_of(c * c_len, c_len), c_len)
            ba = ba_ref[rows, :]
            beta = jax.nn.sigmoid(ba)
            g = neg_rate * jax.nn.softplus(ba + dt_bias)
            g_hi = g.astype(BF16)
            g_lo = (g - g_hi.astype(F32)).astype(BF16)
            gc = _dot(ltri, g_hi) + _dot(ltri, g_lo)
            gc_t = gc.T
            for h in heads:
                cols = slice(h * LANES, (h + 1) * LANES)
                q_h = qn_ref[rows, cols]
                k_h = kn_ref[rows, cols]
                gcol = gc[:, B_HEADS + h:B_HEADS + h + 1]
                grow = gc_t[B_HEADS + h:B_HEADS + h + 1, :]
                glast = gc[c_len - 1:c_len, B_HEADS + h:B_HEADS + h + 1]
                bcol = beta[:, h:h + 1]
                eg = jnp.exp(gcol)
                k_beta = k_h * bcol
                prob.append(dict(
                    c=c, h=h, q_dec=q_h * eg,
                    decay=jnp.where(causal, jnp.exp(jnp.where(causal, gcol - grow, 0.0)), 0.0),
                    lhs=jnp.concatenate([q_h, k_beta], axis=0).astype(BF16), k16=k_h.astype(BF16),
                    rhs=jnp.concatenate([k_beta * eg, vn_ref[rows, cols] * bcol], axis=1).astype(BF16),
                    k_dec=k_h * jnp.exp(glast - gcol), gl=jnp.exp(glast)))
        kks = [_dot_nt(p["lhs"], p["k16"]) for p in prob]
        a_list = [jnp.where(strict, kk[c_len:] * p["decay"], 0.0) for kk, p in zip(kks, prob)]
        t_invs = _unit_lower_inverses(a_list, eye, level_masks)
        wus = [_dot(t.astype(BF16), p["rhs"]).astype(BF16) for t, p in zip(t_invs, prob)]
        kd_wu = [_dot(p["k_dec"].T.astype(BF16), wu) for p, wu in zip(prob, wus)]
        qk_wu = [_dot((kk[:c_len] * p["decay"]).astype(BF16), wu) for p, kk, wu in zip(prob, kks, wus)]
        for p, kd, qk in zip(prob, kd_wu, qk_wu):
            c, h = p["c"], p["h"]
            lhs_ref[c, h] = jnp.concatenate(
                [kd[:, :B_HEAD_DIM], qk[:, :B_HEAD_DIM] - p["q_dec"]], axis=0).astype(BF16)
            ds_ref[c, h] = kd[:, B_HEAD_DIM:]
            o0_ref[c, h] = qk[:, B_HEAD_DIM:]
            gl_ref[c, h] = jnp.broadcast_to(p["gl"], (1, LANES))
        return carry

    lax.fori_loop(0, tb // (c_len * group), prepare, 0)

    def recur(c, carry):
        rows = pl.ds(pl.multiple_of(c * c_len, c_len), c_len)
        states = [state_ref[h] for h in heads]
        prods = [_dot(lhs_ref[c, h], states[h].astype(BF16)) for h in heads]
        for h in heads:
            state_ref[h] = states[h] * gl_ref[c, h] + (ds_ref[c, h] - prods[h][:B_HEAD_DIM])
            raw_ref[rows, h * LANES:(h + 1) * LANES] = o0_ref[c, h] - prods[h][B_HEAD_DIM:]
        return carry

    lax.fori_loop(0, tb // c_len, recur, 0)

    for h in heads:
        cols = slice(h * LANES, (h + 1) * LANES)
        z = z_ref[:, cols]
        o_ref[:, cols] = (_rms(raw_ref[:, cols], out_gain) * (z * jax.nn.sigmoid(z))).astype(BF16)


def _gdn(proj, conv_all, alog_row, dtb_row, out_gain, i_layer, *, bsz, seq, tb=512, group=8):
    t = proj.shape[0]
    nt = seq // tb

    def col(j):
        return pl.BlockSpec((tb, B_WIDTH), lambda b, i: (b * nt + i, j))

    def conv(j):
        return pl.BlockSpec((None, B_CONV, B_WIDTH), lambda b, i: (i_layer, 0, j))

    nc = tb // B_CHUNK
    per_problem = lambda *shape_dtype: pltpu.VMEM((nc, B_HEADS) + shape_dtype[:-1], shape_dtype[-1])
    return pl.pallas_call(
        functools.partial(_gdn_kernel, tb=tb, group=group),
        out_shape=jax.ShapeDtypeStruct((t, B_WIDTH), BF16),
        grid=(bsz, nt),
        in_specs=[col(0), col(1), col(2), col(3),
                  pl.BlockSpec((tb, LANES), lambda b, i: (b * nt + i, 4 * B_WIDTH // LANES)),
                  conv(0), conv(1), conv(2),
                  pl.BlockSpec((1, LANES), lambda b, i: (0, 0)),
                  pl.BlockSpec((1, LANES), lambda b, i: (0, 0)),
                  pl.BlockSpec((1, B_HEAD_DIM), lambda b, i: (0, 0))],
        out_specs=pl.BlockSpec((tb, B_WIDTH), lambda b, i: (b * nt + i, 0)),
        scratch_shapes=[pltpu.VMEM((3, CONV_PAD + tb, B_WIDTH), F32),
                        pltpu.VMEM((tb, B_WIDTH), F32), pltpu.VMEM((tb, B_WIDTH), F32),
                        pltpu.VMEM((tb, B_WIDTH), F32),
                        pltpu.VMEM((B_HEADS, B_HEAD_DIM, B_HEAD_DIM), F32),
                        per_problem(B_HEAD_DIM + B_CHUNK, B_HEAD_DIM, BF16),
                        per_problem(B_HEAD_DIM, B_HEAD_DIM, F32),
                        per_problem(B_CHUNK, B_HEAD_DIM, F32),
                        per_problem(1, LANES, F32),
                        pltpu.VMEM((tb, B_WIDTH), F32)],
        compiler_params=_params("parallel", "arbitrary"),
        name="gdn",
    )(proj, proj, proj, proj, proj, conv_all, conv_all, conv_all, alog_row, dtb_row, out_gain)


def _in_odd_kernel(h_ref, g_ref, w_ref, qg_ref, kg_ref, o_ref, vt_ref, *, tm, tn, tk):
    hn = _rms(h_ref[...], g_ref[...]).astype(BF16)
    lane = lax.broadcasted_iota(jnp.int32, (tm, LANES), 1)
    low = lane < C_HEAD_DIM
    qk_width = 2 * C_HEADS * 2 * C_HEAD_DIM
    for start in range(0, 3 * C_HEADS * C_VALUE_DIM, tn):
        y = _dot(hn, w_ref[:, start:start + tn])
        if start >= qk_width:
            extra = jnp.where(lax.broadcasted_iota(jnp.int32, (V_ROWS - C_VALUE_DIM, tk), 0) == 0, 1.0, 0.0)
            for hh in range(tn // C_VALUE_DIM):
                row0 = ((start - qk_width) // C_VALUE_DIM + hh) * V_ROWS
                for r0 in range(0, tm, MXU_TILE):
                    vt_ref[r0 // tk, row0:row0 + C_VALUE_DIM, r0 % tk:r0 % tk + MXU_TILE] = (
                        y[r0:r0 + MXU_TILE, hh * C_VALUE_DIM:(hh + 1) * C_VALUE_DIM].T.astype(BF16))
                for kb in range(tm // tk):
                    vt_ref[kb, row0 + C_VALUE_DIM:row0 + V_ROWS, :] = extra.astype(BF16)
            continue
        is_q = start < qk_width // 2
        gain = qg_ref[...] if is_q else kg_ref[...]
        for j in range(tn // LANES):
            yb = y[:, j * LANES:(j + 1) * LANES]
            y2 = yb * yb
            lo = jnp.sum(jnp.where(low, y2, 0.0), axis=-1, keepdims=True)
            hi = jnp.sum(jnp.where(low, 0.0, y2), axis=-1, keepdims=True)
            ms = jnp.where(low, lo, hi) * (1.0 / C_HEAD_DIM)
            yn = yb * lax.rsqrt(ms + NORM_EPS) * gain
            if is_q:
                yn = yn * (C_HEAD_DIM ** -0.5 * LOG2_E)
            o_ref[:, start + j * LANES:start + (j + 1) * LANES] = yn.astype(BF16)


def _in_odd(h, gain, w_all, qg, kg, i_layer, *, bsz, seq, tk, tm=512, tn=512):
    t = h.shape[0]
    n = w_all.shape[2]
    nt = seq // tm
    qk_width = 2 * C_HEADS * 2 * C_HEAD_DIM
    return pl.pallas_call(
        functools.partial(_in_odd_kernel, tm=tm, tn=tn, tk=tk),
        out_shape=(jax.ShapeDtypeStruct((t, qk_width), BF16),
                   jax.ShapeDtypeStruct((bsz, seq // tk, C_HEADS * V_ROWS, tk), BF16)),
        grid=(t // tm,),
        in_specs=[pl.BlockSpec((tm, D_MODEL), lambda i: (i, 0)),
                  _resident((1, D_MODEL)), _stacked((D_MODEL, n), i_layer),
                  _resident((1, LANES)), _resident((1, LANES))],
        out_specs=(pl.BlockSpec((tm, qk_width), lambda i: (i, 0)),
                   pl.BlockSpec((None, tm // tk, C_HEADS * V_ROWS, tk), lambda i: (i // nt, i % nt, 0, 0))),
        compiler_params=_params("parallel"),
        name="in_odd",
    )(h, gain, w_all, qg, kg)


def _attn_kernel(slopes_ref, q_ref, k_ref, vt_ref, lam_ref, sg_ref, o_ref, s0_ref, s1_ref, mx0_ref,
                 mx1_ref, p_ref, q2_ref, rel_ref, m_ref, acc_ref, *, tq, tk, lam_init):
    slope = slopes_ref[pl.program_id(1)] * LOG2_E
    nq = q_ref.shape[0] // tq
    ki = lax.broadcasted_iota(jnp.int32, (tk, 2 * tq), 0)
    qj = lax.broadcasted_iota(jnp.int32, (tk, 2 * tq), 1)
    qj = jnp.where(qj >= tq, qj - tq, qj)
    rel_ref[...] = (ki - qj).astype(F32) * slope
    lane = lax.broadcasted_iota(jnp.int32, (tq, LANES), 1)
    lp = lam_ref[...]
    lam = (jnp.exp(jnp.sum(lp[0:1] * lp[1:2], axis=-1, keepdims=True))
           - jnp.exp(jnp.sum(lp[2:3] * lp[3:4], axis=-1, keepdims=True)) + lam_init)
    sub_gain = sg_ref[...]

    def load_q2(qi):
        q = q_ref[pl.ds(pl.multiple_of(qi * tq, tq), tq), :]
        zero = jnp.zeros_like(q)
        return jnp.concatenate([jnp.where(lane < C_HEAD_DIM, q, zero),
                                jnp.where(lane < C_HEAD_DIM, zero, q)], axis=0)

    lanes2 = 2 * tq
    n_tile = lanes2 // MXU_TILE
    n_slab = tk // MXU_TILE
    s_bufs = (s0_ref, s1_ref)
    mx_bufs = (mx0_ref, mx1_ref)
    chunk = ATTN_ROW_CHUNK

    def fold_max(x):
        return jnp.max(x.reshape(x.shape[0] // SUBLANES, SUBLANES, x.shape[1]), axis=0)

    def exp_rows(shift, par, rows, cols):
        p_ref[rows, cols] = jnp.exp2((s_bufs[par][rows, cols] + shift[:, cols]).astype(BF16))

    def produce_scores(par, k_blk, q2, j):
        cols = slice(j * MXU_TILE, (j + 1) * MXU_TILE)
        t = _dot_nt(k_blk, q2[cols, :]) + rel_ref[:, cols]
        s_bufs[par][:, cols] = t
        mx_bufs[par][:, cols] = fold_max(t)

    q2_first = load_q2(0)
    q2_ref[...] = q2_first
    for j in range(n_tile):
        produce_scores(0, k_ref[0:tk, :], q2_first, j)

    def q_block(qi, base):
        q0 = pl.multiple_of(qi * tq, tq)
        q2 = q2_ref[...]
        m_ref[...] = jnp.full(m_ref.shape, -jnp.inf, F32)
        acc_ref[...] = jnp.zeros(acc_ref.shape, F32)

        def step(kb, last, par):
            if last:
                q2_next = load_q2(jnp.minimum(qi + 1, nq - 1))
                q2_ref[...] = q2_next
                k_next = k_ref[0:tk, :]
            else:
                q2_next = q2
                k_next = k_ref[pl.ds(pl.multiple_of((kb + 1) * tk, tk), tk), :]

            def scores_piece(j):
                produce_scores(1 - par, k_next, q2_next, j)

            def live_slabs(j):
                if not last:
                    return n_slab
                first_query = (j * MXU_TILE) % tq
                return first_query // MXU_TILE + 1

            scores_piece(0)
            if last:
                mx_tiles = []
                for j in range(n_tile):
                    cols = slice(j * MXU_TILE, (j + 1) * MXU_TILE)
                    mx_j = jnp.full((SUBLANES, MXU_TILE), -jnp.inf, F32)
                    for r0 in range(0, live_slabs(j) * MXU_TILE, chunk):
                        krow = lax.broadcasted_iota(jnp.int32, (chunk, MXU_TILE), 0) + r0
                        qcol = lax.broadcasted_iota(jnp.int32, (chunk, MXU_TILE), 1) + (j * MXU_TILE) % tq
                        t = jnp.where(krow <= qcol, s_bufs[par][r0:r0 + chunk, cols], -jnp.inf)
                        s_bufs[par][r0:r0 + chunk, cols] = t
                        mx_j = jnp.maximum(mx_j, fold_max(t))
                    mx_tiles.append(mx_j)
                mx = jnp.concatenate(mx_tiles, axis=1)
            else:
                mx = mx_bufs[par][...]
            offset = slope * (kb * tk - q0).astype(F32)
            m_prev = m_ref[...]
            m_new = jnp.maximum(m_prev, jnp.max(mx, axis=0, keepdims=True) + offset)
            alpha = jnp.exp2(m_prev - m_new)
            shift = offset - m_new
            pv = [None] * n_tile
            for slab in range(n_slab):
                keys = slice(slab * MXU_TILE, (slab + 1) * MXU_TILE)
                tiles = [j for j in range(n_tile) if slab < live_slabs(j)]
                for r0 in range(keys.start, keys.stop, chunk):
                    for j in tiles:
                        exp_rows(shift, par, slice(r0, r0 + chunk), slice(j * MXU_TILE, (j + 1) * MXU_TILE))
                first = 1 + slab * (n_tile - 1) // n_slab
                for j in range(first, 1 + (slab + 1) * (n_tile - 1) // n_slab):
                    scores_piece(j)
                for j in tiles:
                    part = _dot(vt_ref[kb][:, keys], p_ref[keys, j * MXU_TILE:(j + 1) * MXU_TILE])
                    pv[j] = part if pv[j] is None else pv[j] + part
            acc_ref[...] = alpha * acc_ref[...] + jnp.concatenate(pv, axis=1)
            m_ref[...] = m_new

        def either_parity(kb, last):
            dyn_par = (base + kb) % 2
            for par in range(2):
                pl.when(dyn_par == par)(functools.partial(step, kb, last, par))

        def body(kb, c):
            either_parity(kb, False)
            return c

        lax.fori_loop(0, qi, body, 0)
        either_parity(qi, True)
        acc = acc_ref[...]
        o2 = acc[:C_VALUE_DIM] / acc[C_VALUE_DIM:C_VALUE_DIM + 1]
        o_t = o2[:, :tq] - lam * o2[:, tq:]
        o_t = o_t * lax.rsqrt(jnp.mean(o_t * o_t, axis=0, keepdims=True) + NORM_EPS)
        o_ref[pl.ds(q0, tq), :] = (o_t.T * sub_gain * (1.0 - lam_init)).astype(BF16)
        return (base + qi + 1) % 2

    lax.fori_loop(0, nq, q_block, jnp.int32(0))


def _attn(qk, vt, slopes, lam_params, sub_gain, *, bsz, seq, lam_init, tq):
    t = qk.shape[0]
    tk = vt.shape[3]
    assert tk == tq
    kernel = functools.partial(_attn_kernel, tq=tq, tk=tk, lam_init=lam_init)
    return pl.pallas_call(
        kernel,
        out_shape=jax.ShapeDtypeStruct((t, C_HEADS * C_VALUE_DIM), BF16),
        grid_spec=pltpu.PrefetchScalarGridSpec(
            num_scalar_prefetch=1,
            grid=(bsz, C_HEADS),
            in_specs=[pl.BlockSpec((seq, LANES), lambda b, h, s: (b, h)),
                      pl.BlockSpec((seq, LANES), lambda b, h, s: (b, C_HEADS + h)),
                      pl.BlockSpec((None, seq // tk, V_ROWS, tk), lambda b, h, s: (b, 0, h, 0)),
                      pl.BlockSpec((4, C_HEAD_DIM), lambda b, h, s: (0, 0)),
                      pl.BlockSpec((1, C_VALUE_DIM), lambda b, h, s: (0, 0))],
            out_specs=pl.BlockSpec((seq, LANES), lambda b, h, s: (b, h)),
            scratch_shapes=[pltpu.VMEM((tk, 2 * tq), F32), pltpu.VMEM((tk, 2 * tq), F32),
                            pltpu.VMEM((SUBLANES, 2 * tq), F32), pltpu.VMEM((SUBLANES, 2 * tq), F32),
                            pltpu.VMEM((tk, 2 * tq), BF16),
                            pltpu.VMEM((2 * tq, LANES), BF16),
                            pltpu.VMEM((tk, 2 * tq), F32),
                            pltpu.VMEM((1, 2 * tq), F32),
                            pltpu.VMEM((V_ROWS, 2 * tq), F32)]),
        compiler_params=_params("parallel", "parallel"),
        name="diff_attn",
    )(slopes, qk, qk, vt, lam_params, sub_gain)


def _post_kernel(*refs, n_mix, layer, fc):
    h_ref = refs[0]
    mix_refs = refs[1:1 + n_mix]
    (w_out_ref, g_mlp_ref, w1_ref, w2_ref, g_ple_ref, w_gate_ref, w_proj_ref, p_ref, o_ref) = refs[1 + n_mix:]
    del layer
    mix = mix_refs[0][...] if n_mix == 1 else jnp.concatenate([m[...] for m in mix_refs], axis=1)
    h = h_ref[...] + _dot(mix, w_out_ref[...])
    hn = _rms(h, g_mlp_ref[...]).astype(BF16)
    acc = h
    for c in range(0, D_FF, fc):
        a = jnp.maximum(_dot(hn, w1_ref[:, c:c + fc]), 0.0)
        acc = acc + _dot((a * a).astype(BF16), w2_ref[c:c + fc, :])
    h = acc
    gate = jax.nn.sigmoid(_dot(_rms(h, g_ple_ref[...]).astype(BF16), w_gate_ref[...]))
    o_ref[...] = h + _dot(p_ref[...].astype(BF16), w_proj_ref[...]) * gate


def _post(h, mixes, w_out_all, i_mix, g_mlp, w1_all, w2_all, g_ple, w_gate_all, w_proj_all, p_all, layer,
          *, tm=512, fc=512):
    t = h.shape[0]
    row = lambda i: (i, 0)
    mix_specs = [pl.BlockSpec((tm, m.shape[1]), row) for m in mixes]
    return pl.pallas_call(
        functools.partial(_post_kernel, n_mix=len(mixes), layer=layer, fc=fc),
        out_shape=jax.ShapeDtypeStruct((t, D_MODEL), F32),
        grid=(t // tm,),
        in_specs=[pl.BlockSpec((tm, D_MODEL), row)] + mix_specs + [
            _stacked((D_MODEL, D_MODEL), i_mix), _resident((1, D_MODEL)),
            _stacked((D_MODEL, D_FF), layer), _stacked((D_FF, D_MODEL), layer),
            _resident((1, D_MODEL)), _stacked((D_MODEL, D_MODEL), layer),
            _stacked((PLE_DIM, D_MODEL), layer),
            pl.BlockSpec((None, tm, PLE_DIM), lambda i: (layer, i, 0))],
        out_specs=pl.BlockSpec((tm, D_MODEL), row),
        compiler_params=_params("parallel"),
        name="post",
    )(h, *mixes, w_out_all, g_mlp, w1_all, w2_all, g_ple, w_gate_all, w_proj_all, p_all)


def _lane_row(values, offset):
    return jnp.zeros((1, LANES), F32).at[0, offset:offset + values.shape[0]].set(values.astype(F32))


def kernel(x, p, ln_mix_e, w_in_e, gmlp_v_gain, gmlp_ws, gmlp_bs, gdn_conv, gdn_a_log, gdn_dt_bias,
           gdn_out_gain, w_out_e, ln_mix_o, w_qkv_o, attn_q_gain, attn_k_gain, diff_lambda,
           attn_sub_gain, w_out_o, ln_mlp, w_mlp1, w_mlp2, ln_ple, w_ple_gate, w_ple_proj):
    bsz, seq, d = x.shape
    depth = p.shape[0]
    t = bsz * seq
    h = x.reshape(t, d)
    p_all = p.reshape(depth, t, PLE_DIM)
    slopes = 2.0 ** (-8.0 * jnp.arange(1, C_HEADS + 1, dtype=F32) / C_HEADS)
    row = lambda a: a.reshape(1, -1).astype(F32)
    w_in_e16 = jnp.pad(w_in_e, ((0, 0), (0, 0), (0, EVEN_PAD - w_in_e.shape[2]))).astype(BF16)
    w_qkv_o16 = w_qkv_o.astype(BF16)
    w_out_e16, w_out_o16 = w_out_e.astype(BF16), w_out_o.astype(BF16)
    w_mlp1_16, w_mlp2_16 = w_mlp1.astype(BF16), w_mlp2.astype(BF16)
    w_gate16, w_proj16 = w_ple_gate.astype(BF16), w_ple_proj.astype(BF16)
    for layer in range(depth):
        i = layer // 2
        if layer % 2 == 0:
            a_out, proj = _in_even(h, row(ln_mix_e[i]), w_in_e16, row(gmlp_v_gain[i]), gmlp_ws,
                                   gmlp_bs[i].T, i)
            b_out = _gdn(proj, gdn_conv, _lane_row(gdn_a_log[i], B_HEADS),
                         _lane_row(gdn_dt_bias[i], B_HEADS), row(gdn_out_gain[i]), i, bsz=bsz, seq=seq)
            mixes = (a_out, b_out)
            w_out = w_out_e16
        else:
            lam_init = 0.8 - 0.6 * math.exp(-0.3 * layer)
            qg = row(jnp.concatenate([attn_q_gain[i], attn_q_gain[i]]))
            kg = row(jnp.concatenate([attn_k_gain[i], attn_k_gain[i]]))
            qk, vt = _in_odd(h, row(ln_mix_o[i]), w_qkv_o16, qg, kg, i, bsz=bsz, seq=seq, tk=ATTN_BLOCK)
            o = _attn(qk, vt, slopes, diff_lambda[i], row(attn_sub_gain[i]), bsz=bsz, seq=seq,
                      lam_init=lam_init, tq=ATTN_BLOCK)
            mixes = (o,)
            w_out = w_out_o16
        h = _post(h, mixes, w_out, i, row(ln_mlp[layer]), w_mlp1_16, w_mlp2_16, row(ln_ple[layer]),
                  w_gate16, w_proj16, p_all, layer)
    return h.reshape(bsz, seq, d)
```

```python
import functools
import math

import jax
import jax.numpy as jnp
from jax import lax
from jax.experimental import pallas as pl
from jax.experimental.pallas import tpu as pltpu

F32 = jnp.float32
BF16 = jnp.bfloat16
NORM_EPS = 1e-6

LANES = 128
SUBLANES = 8
MXU_TILE = 256
D_MODEL = 1024
PLE_DIM = 256
D_FF = 4 * D_MODEL
A_WIDTH = 512
A_GROUPS = 4
A_CHUNK = 128
B_HEADS = 4
B_HEAD_DIM = 128
B_WIDTH = 512
B_CONV = 4
B_CHUNK = 64
CONV_PAD = 8
C_HEADS = 8
C_HEAD_DIM = 64
C_VALUE_DIM = 128
ATTN_Q_BLOCK = 512
ATTN_K_BLOCK = 512
ATTN_ROW_CHUNK = 32
V_ROWS = C_VALUE_DIM + 16
LOG2_E = math.log2(math.e)
EVEN_MAIN = 2 * A_WIDTH + 4 * B_WIDTH
EVEN_PAD = EVEN_MAIN + LANES
VMEM_LIMIT = 56 * 1024 * 1024


def _rms(x, gain):
    return x * lax.rsqrt(jnp.mean(x * x, axis=-1, keepdims=True) + NORM_EPS) * gain


def _dot(a, b):
    return jnp.dot(a, b, preferred_element_type=F32)


def _dot_nt(a, b):
    return lax.dot_general(a, b, (((1,), (1,)), ((), ())), preferred_element_type=F32)


def _resident(shape):
    zeros = (0,) * len(shape)
    return pl.BlockSpec(shape, lambda *_: zeros, pipeline_mode=pl.Buffered(1))


def _stacked(shape, index):
    zeros = (0,) * len(shape)
    return pl.BlockSpec((None,) + shape, lambda *_: (index,) + zeros, pipeline_mode=pl.Buffered(1))


def _params(*sem):
    return pltpu.CompilerParams(dimension_semantics=sem, vmem_limit_bytes=VMEM_LIMIT)


def _in_even_kernel(h_ref, g_ref, w_ref, vg_ref, ws_ref, bs_ref, cw_ref, a_ref, o_ref, buf_ref, *,
                    tm, tiles_per_seq):
    @pl.when(pl.program_id(0) % tiles_per_seq == 0)
    def _():
        buf_ref[:, 0:CONV_PAD, :] = jnp.zeros((3, CONV_PAD, B_WIDTH), F32)

    hn = _rms(h_ref[...], g_ref[...]).astype(BF16)
    u = jax.nn.gelu(_dot(hn, w_ref[:, 0:A_WIDTH]))
    v = jax.nn.gelu(_dot(hn, w_ref[:, A_WIDTH:2 * A_WIDTH]))
    for part in range(3):
        cols = slice(part * B_WIDTH, (part + 1) * B_WIDTH)
        x = _dot(hn, w_ref[:, 2 * A_WIDTH + cols.start:2 * A_WIDTH + cols.stop])
        buf_ref[part, CONV_PAD:CONV_PAD + tm, :] = x
        cw = cw_ref[:, cols]
        y = cw[B_CONV - 1:B_CONV, :] * x
        for j in range(B_CONV - 1):
            off = CONV_PAD - (B_CONV - 1) + j
            y = y + cw[j:j + 1, :] * buf_ref[part, off:off + tm, :]
        buf_ref[part, 0:CONV_PAD, :] = x[tm - CONV_PAD:tm, :]
        half = 0.5 * y
        y = half + half * jnp.tanh(half)
        if part == 2:
            o_ref[:, cols] = y
        else:
            for h in range(B_HEADS):
                hcols = slice(cols.start + h * LANES, cols.start + (h + 1) * LANES)
                yh = y[:, h * LANES:(h + 1) * LANES]
                yh = yh * lax.rsqrt(jnp.sum(yh * yh, axis=-1, keepdims=True) + NORM_EPS)
                o_ref[:, hcols] = yh * (B_HEAD_DIM ** -0.5) if part == 0 else yh
    rest = 2 * A_WIDTH + 3 * B_WIDTH
    o_ref[:, 3 * B_WIDTH:] = _dot(hn, w_ref[:, rest:])
    ii = lax.broadcasted_iota(jnp.int32, (A_CHUNK, A_CHUNK), 0)
    jj = lax.broadcasted_iota(jnp.int32, (A_CHUNK, A_CHUNK), 1)
    for g in range(A_GROUPS):
        cols = slice(g * LANES, (g + 1) * LANES)
        w = jnp.where(ii >= jj, ws_ref[g], 0.0).astype(BF16)
        bias = bs_ref[:, g:g + 1]
        gain = vg_ref[:, cols]
        for c in range(tm // A_CHUNK):
            rows = slice(c * A_CHUNK, (c + 1) * A_CHUNK)
            vg = v[rows, cols]
            vc = vg - jnp.mean(vg, axis=-1, keepdims=True)
            y = vc * lax.rsqrt(jnp.mean(vc * vc, axis=-1, keepdims=True) + NORM_EPS) * gain
            s = _dot(w, y.astype(BF16)) + bias
            a_ref[rows, cols] = (u[rows, cols] * s).astype(BF16)


def _in_even(h, gain, w_all, v_gain, ws_all, bs_t, conv_all, i, *, seq, tm=512):
    t = h.shape[0]
    return pl.pallas_call(
        functools.partial(_in_even_kernel, tm=tm, tiles_per_seq=seq // tm),
        out_shape=(jax.ShapeDtypeStruct((t, A_WIDTH), BF16),
                   jax.ShapeDtypeStruct((t, EVEN_PAD - 2 * A_WIDTH), F32)),
        grid=(t // tm,),
        in_specs=[pl.BlockSpec((tm, D_MODEL), lambda i_: (i_, 0)),
                  _resident((1, D_MODEL)), _stacked((D_MODEL, EVEN_PAD), i),
                  _resident((1, A_WIDTH)), _stacked((A_GROUPS, A_CHUNK, A_CHUNK), i),
                  _resident((A_CHUNK, A_GROUPS)), _stacked((B_CONV, 3 * B_WIDTH), i)],
        out_specs=(pl.BlockSpec((tm, A_WIDTH), lambda i_: (i_, 0)),
                   pl.BlockSpec((tm, EVEN_PAD - 2 * A_WIDTH), lambda i_: (i_, 0))),
        scratch_shapes=[pltpu.VMEM((3, CONV_PAD + tm, B_WIDTH), F32)],
        compiler_params=_params("arbitrary"),
        name="in_even",
    )(h, gain, w_all, v_gain, ws_all, bs_t, conv_all)


def _unit_lower_inverses(a_list, eye, level_masks):
    ds = [eye - jnp.where(level_masks[0], a, 0.0) for a in a_list]
    for mask in level_masks[1:]:
        d16 = [d.astype(BF16) for d in ds]
        mds = [_dot(jnp.where(mask, a, 0.0).astype(BF16), d) for a, d in zip(a_list, d16)]
        ds = [d - _dot(d_lo, md.astype(BF16)) for d, d_lo, md in zip(ds, d16, mds)]
    return ds


def _gdn_kernel(qn_ref, kn_ref, vn_ref, z_ref, ba_ref, alog_ref, dtb_ref, og_ref, o_ref, state_ref,
                lhs_ref, ds_ref, o0_ref, gl_ref, raw_ref, *, tb, group):
    c_len = B_CHUNK

    @pl.when(pl.program_id(1) == 0)
    def _():
        state_ref[...] = jnp.zeros_like(state_ref)

    ii = lax.broadcasted_iota(jnp.int32, (c_len, c_len), 0)
    jj = lax.broadcasted_iota(jnp.int32, (c_len, c_len), 1)
    causal = ii >= jj
    strict = ii > jj
    eye = jnp.where(ii == jj, 1.0, 0.0).astype(F32)
    ltri = jnp.where(causal, 1.0, 0.0).astype(BF16)
    level_masks = []
    s = 1
    while s < c_len:
        level_masks.append(((ii // (2 * s)) == (jj // (2 * s))) & ((ii % (2 * s)) >= s) & ((jj % (2 * s)) < s))
        s *= 2
    neg_rate = -jnp.exp(alog_ref[...])
    dt_bias = dtb_ref[...]
    out_gain = og_ref[...]

    heads = range(B_HEADS)

    def prepare(gi, carry):
        prob = []
        for cc in range(group):
            c = gi * group + cc
            rows = pl.ds(pl.multiple_of(c * c_len, c_len), c_len)
            ba = ba_ref[rows, :]
            beta = jax.nn.sigmoid(ba)
            g = neg_rate * jax.nn.softplus(ba + dt_bias)
            g_hi = g.astype(BF16)
            g_lo = (g - g_hi.astype(F32)).astype(BF16)
            gc = _dot(ltri, g_hi) + _dot(ltri, g_lo)
            gc_t = gc.T
            for h in heads:
                cols = slice(h * LANES, (h + 1) * LANES)
                q_h = qn_ref[rows, cols]
                k_h = kn_ref[rows, cols]
                gcol = gc[:, B_HEADS + h:B_HEADS + h + 1]
                grow = gc_t[B_HEADS + h:B_HEADS + h + 1, :]
                glast = gc[c_len - 1:c_len, B_HEADS + h:B_HEADS + h + 1]
                bcol = beta[:, h:h + 1]
                eg = jnp.exp(gcol)
                k_beta = k_h * bcol
                prob.append(dict(
                    c=c, h=h, q_dec=q_h * eg,
                    decay=jnp.where(causal, jnp.exp(jnp.where(causal, gcol - grow, 0.0)), 0.0),
                    lhs=jnp.concatenate([q_h, k_beta], axis=0).astype(BF16), k16=k_h.astype(BF16),
                    rhs=jnp.concatenate([k_beta * eg, vn_ref[rows, cols] * bcol], axis=1).astype(BF16),
                    k_dec=k_h * jnp.exp(glast - gcol), gl=jnp.exp(glast)))
        kks = [_dot_nt(p["lhs"], p["k16"]) for p in prob]
        a_list = [jnp.where(strict, kk[c_len:] * p["decay"], 0.0) for kk, p in zip(kks, prob)]
        t_invs = _unit_lower_inverses(a_list, eye, level_masks)
        wus = [_dot(t.astype(BF16), p["rhs"]).astype(BF16) for t, p in zip(t_invs, prob)]
        kd_wu = [_dot(p["k_dec"].T.astype(BF16), wu) for p, wu in zip(prob, wus)]
        qk_wu = [_dot((kk[:c_len] * p["decay"]).astype(BF16), wu) for p, kk, wu in zip(prob, kks, wus)]
        for p, kd, qk in zip(prob, kd_wu, qk_wu):
            c, h = p["c"], p["h"]
            lhs_ref[c, h] = jnp.concatenate(
                [kd[:, :B_HEAD_DIM], qk[:, :B_HEAD_DIM] - p["q_dec"]], axis=0).astype(BF16)
            ds_ref[c, h] = kd[:, B_HEAD_DIM:]
            o0_ref[c, h] = qk[:, B_HEAD_DIM:]
            gl_ref[c, h] = jnp.broadcast_to(p["gl"], (1, LANES))
        return carry

    lax.fori_loop(0, tb // (c_len * group), prepare, 0)

    def recur(c, carry):
        rows = pl.ds(pl.multiple_of(c * c_len, c_len), c_len)
        states = [state_ref[h] for h in heads]
        prods = [_dot(lhs_ref[c, h], states[h].astype(BF16)) for h in heads]
        for h in heads:
            state_ref[h] = states[h] * gl_ref[c, h] + (ds_ref[c, h] - prods[h][:B_HEAD_DIM])
            raw_ref[rows, h * LANES:(h + 1) * LANES] = o0_ref[c, h] - prods[h][B_HEAD_DIM:]
        return carry

    lax.fori_loop(0, tb // c_len, recur, 0)

    for h in heads:
        cols = slice(h * LANES, (h + 1) * LANES)
        z = z_ref[:, cols]
        o_ref[:, cols] = (_rms(raw_ref[:, cols], out_gain) * (z * jax.nn.sigmoid(z))).astype(BF16)


def _gdn(proj, alog_row, dtb_row, out_gain, *, bsz, seq, tb=512, group=8):
    t = proj.shape[0]
    nt = seq // tb

    def col(j):
        return pl.BlockSpec((tb, B_WIDTH), lambda b, i: (b * nt + i, j))

    nc = tb // B_CHUNK
    per_problem = lambda *shape_dtype: pltpu.VMEM((nc, B_HEADS) + shape_dtype[:-1], shape_dtype[-1])
    return pl.pallas_call(
        functools.partial(_gdn_kernel, tb=tb, group=group),
        out_shape=jax.ShapeDtypeStruct((t, B_WIDTH), BF16),
        grid=(bsz, nt),
        in_specs=[col(0), col(1), col(2), col(3),
                  pl.BlockSpec((tb, LANES), lambda b, i: (b * nt + i, 4 * B_WIDTH // LANES)),
                  pl.BlockSpec((1, LANES), lambda b, i: (0, 0)),
                  pl.BlockSpec((1, LANES), lambda b, i: (0, 0)),
                  pl.BlockSpec((1, B_HEAD_DIM), lambda b, i: (0, 0))],
        out_specs=pl.BlockSpec((tb, B_WIDTH), lambda b, i: (b * nt + i, 0)),
        scratch_shapes=[pltpu.VMEM((B_HEADS, B_HEAD_DIM, B_HEAD_DIM), F32),
                        per_problem(B_HEAD_DIM + B_CHUNK, B_HEAD_DIM, BF16),
                        per_problem(B_HEAD_DIM, B_HEAD_DIM, F32),
                        per_problem(B_CHUNK, B_HEAD_DIM, F32),
                        per_problem(1, LANES, F32),
                        pltpu.VMEM((tb, B_WIDTH), F32)],
        compiler_params=_params("parallel", "arbitrary"),
        name="gdn",
    )(proj, proj, proj, proj, proj, alog_row, dtb_row, out_gain)


def _in_odd_kernel(h_ref, g_ref, w_ref, qg_ref, kg_ref, o_ref, vt_ref, *, tm, tn, tk):
    hn = _rms(h_ref[...], g_ref[...]).astype(BF16)
    lane = lax.broadcasted_iota(jnp.int32, (tm, LANES), 1)
    low = lane < C_HEAD_DIM
    qk_width = 2 * C_HEADS * 2 * C_HEAD_DIM
    for start in range(0, 3 * C_HEADS * C_VALUE_DIM, tn):
        y = _dot(hn, w_ref[:, start:start + tn])
        if start >= qk_width:
            extra = jnp.where(lax.broadcasted_iota(jnp.int32, (V_ROWS - C_VALUE_DIM, tk), 0) == 0, 1.0, 0.0)
            for hh in range(tn // C_VALUE_DIM):
                row0 = ((start - qk_width) // C_VALUE_DIM + hh) * V_ROWS
                for r0 in range(0, tm, MXU_TILE):
                    vt_ref[r0 // tk, row0:row0 + C_VALUE_DIM, r0 % tk:r0 % tk + MXU_TILE] = (
                        y[r0:r0 + MXU_TILE, hh * C_VALUE_DIM:(hh + 1) * C_VALUE_DIM].T.astype(BF16))
                for kb in range(tm // tk):
                    vt_ref[kb, row0 + C_VALUE_DIM:row0 + V_ROWS, :] = extra.astype(BF16)
            continue
        is_q = start < qk_width // 2
        gain = qg_ref[...] if is_q else kg_ref[...]
        for j in range(tn // LANES):
            yb = y[:, j * LANES:(j + 1) * LANES]
            y2 = yb * yb
            lo = jnp.sum(jnp.where(low, y2, 0.0), axis=-1, keepdims=True)
            hi = jnp.sum(jnp.where(low, 0.0, y2), axis=-1, keepdims=True)
            ms = jnp.where(low, lo, hi) * (1.0 / C_HEAD_DIM)
            yn = yb * lax.rsqrt(ms + NORM_EPS) * gain
            if is_q:
                yn = yn * (C_HEAD_DIM ** -0.5 * LOG2_E)
            o_ref[:, start + j * LANES:start + (j + 1) * LANES] = yn.astype(BF16)


def _in_odd(h, gain, w_all, qg, kg, i_layer, *, bsz, seq, tk, tm=512, tn=512):
    t = h.shape[0]
    n = w_all.shape[2]
    nt = seq // tm
    qk_width = 2 * C_HEADS * 2 * C_HEAD_DIM
    return pl.pallas_call(
        functools.partial(_in_odd_kernel, tm=tm, tn=tn, tk=tk),
        out_shape=(jax.ShapeDtypeStruct((t, qk_width), BF16),
                   jax.ShapeDtypeStruct((bsz, seq // tk, C_HEADS * V_ROWS, tk), BF16)),
        grid=(t // tm,),
        in_specs=[pl.BlockSpec((tm, D_MODEL), lambda i: (i, 0)),
                  _resident((1, D_MODEL)), _stacked((D_MODEL, n), i_layer),
                  _resident((1, LANES)), _resident((1, LANES))],
        out_specs=(pl.BlockSpec((tm, qk_width), lambda i: (i, 0)),
                   pl.BlockSpec((None, tm // tk, C_HEADS * V_ROWS, tk), lambda i: (i // nt, i % nt, 0, 0))),
        compiler_params=_params("parallel"),
        name="in_odd",
    )(h, gain, w_all, qg, kg)


def _attn_kernel(slopes_ref, q_ref, k_ref, vt_ref, lam_ref, sg_ref, o_ref, s0_ref, s1_ref, mx0_ref,
                 mx1_ref, p_ref, q2_ref, rel_ref, m_ref, acc_ref, *, tq, tk, lam_init):
    slope = slopes_ref[pl.program_id(1)] * LOG2_E
    nq = q_ref.shape[0] // tq
    ki = lax.broadcasted_iota(jnp.int32, (tk, 2 * tq), 0)
    qj = lax.broadcasted_iota(jnp.int32, (tk, 2 * tq), 1)
    qj = jnp.where(qj >= tq, qj - tq, qj)
    rel_ref[...] = (ki - qj).astype(F32) * slope
    lane = lax.broadcasted_iota(jnp.int32, (tq, LANES), 1)
    lp = lam_ref[...]
    lam = (jnp.exp(jnp.sum(lp[0:1] * lp[1:2], axis=-1, keepdims=True))
           - jnp.exp(jnp.sum(lp[2:3] * lp[3:4], axis=-1, keepdims=True)) + lam_init)
    sub_gain = sg_ref[...]

    def load_q2(qi):
        q = q_ref[pl.ds(pl.multiple_of(qi * tq, tq), tq), :]
        zero = jnp.zeros_like(q)
        return jnp.concatenate([jnp.where(lane < C_HEAD_DIM, q, zero),
                                jnp.where(lane < C_HEAD_DIM, zero, q)], axis=0)

    lanes2 = 2 * tq
    n_tile = lanes2 // MXU_TILE
    n_slab = tk // MXU_TILE
    n_diag = tq // tk
    s_bufs = (s0_ref, s1_ref)
    mx_bufs = (mx0_ref, mx1_ref)
    chunk = ATTN_ROW_CHUNK

    def fold_max(x):
        return jnp.max(x.reshape(x.shape[0] // SUBLANES, SUBLANES, x.shape[1]), axis=0)

    def exp_rows(shift, par, rows, cols):
        p_ref[rows, cols] = jnp.exp2((s_bufs[par][rows, cols] + shift[:, cols]).astype(BF16))

    def produce_scores(par, k_blk, q2, j):
        cols = slice(j * MXU_TILE, (j + 1) * MXU_TILE)
        t = _dot_nt(k_blk, q2[cols, :]) + rel_ref[:, cols]
        s_bufs[par][:, cols] = t
        mx_bufs[par][:, cols] = fold_max(t)

    q2_first = load_q2(0)
    q2_ref[...] = q2_first
    for j in range(n_tile):
        produce_scores(0, k_ref[0:tk, :], q2_first, j)

    def q_block(qi, base):
        q0 = pl.multiple_of(qi * tq, tq)
        q2 = q2_ref[...]
        m_ref[...] = jnp.full(m_ref.shape, -jnp.inf, F32)
        acc_ref[...] = jnp.zeros(acc_ref.shape, F32)

        def step(kb, diag, par):
            if diag == n_diag - 1:
                q2_next = load_q2(jnp.minimum(qi + 1, nq - 1))
                q2_ref[...] = q2_next
                k_next = k_ref[0:tk, :]
            else:
                q2_next = q2
                k_next = k_ref[pl.ds(pl.multiple_of((kb + 1) * tk, tk), tk), :]

            def scores_piece(j):
                produce_scores(1 - par, k_next, q2_next, j)

            def lead(j):
                return (j * MXU_TILE) % tq - diag * tk

            def live_slabs(j):
                if diag is None:
                    return n_slab
                return min(max(lead(j) // MXU_TILE + 1, 0), n_slab)

            scores_piece(0)
            if diag is None:
                mx = mx_bufs[par][...]
            else:
                mx_tiles = []
                for j in range(n_tile):
                    cols = slice(j * MXU_TILE, (j + 1) * MXU_TILE)
                    mx_j = jnp.full((SUBLANES, MXU_TILE), -jnp.inf, F32)
                    for r0 in range(0, live_slabs(j) * MXU_TILE, chunk):
                        t = s_bufs[par][r0:r0 + chunk, cols]
                        if r0 // MXU_TILE == lead(j) // MXU_TILE:
                            krow = lax.broadcasted_iota(jnp.int32, (chunk, MXU_TILE), 0) + r0
                            qcol = lax.broadcasted_iota(jnp.int32, (chunk, MXU_TILE), 1) + lead(j)
                            t = jnp.where(krow <= qcol, t, -jnp.inf)
                            s_bufs[par][r0:r0 + chunk, cols] = t
                        mx_j = jnp.maximum(mx_j, fold_max(t))
                    mx_tiles.append(mx_j)
                mx = jnp.concatenate(mx_tiles, axis=1)
            offset = slope * (kb * tk - q0).astype(F32)
            m_prev = m_ref[...]
            m_new = jnp.maximum(m_prev, jnp.max(mx, axis=0, keepdims=True) + offset)
            alpha = jnp.exp2(m_prev - m_new)
            shift = offset - m_new
            pv = [None] * n_tile
            for slab in range(n_slab):
                keys = slice(slab * MXU_TILE, (slab + 1) * MXU_TILE)
                tiles = [j for j in range(n_tile) if slab < live_slabs(j)]
                for r0 in range(keys.start, keys.stop, chunk):
                    for j in tiles:
                        exp_rows(shift, par, slice(r0, r0 + chunk), slice(j * MXU_TILE, (j + 1) * MXU_TILE))
                first = 1 + slab * (n_tile - 1) // n_slab
                for j in range(first, 1 + (slab + 1) * (n_tile - 1) // n_slab):
                    scores_piece(j)
                for j in tiles:
                    part = _dot(vt_ref[kb][:, keys], p_ref[keys, j * MXU_TILE:(j + 1) * MXU_TILE])
                    pv[j] = part if pv[j] is None else pv[j] + part
            dead = jnp.zeros((V_ROWS, MXU_TILE), F32)
            pv = [dead if part is None else part for part in pv]
            acc_ref[...] = alpha * acc_ref[...] + jnp.concatenate(pv, axis=1)
            m_ref[...] = m_new

        def either_parity(kb, diag):
            dyn_par = (base + kb) % 2
            for par in range(2):
                pl.when(dyn_par == par)(functools.partial(step, kb, diag, par))

        def body(kb, c):
            either_parity(kb, None)
            return c

        lax.fori_loop(0, qi * n_diag, body, 0)
        for diag in range(n_diag):
            either_parity(qi * n_diag + diag, diag)
        acc = acc_ref[...]
        o2 = acc[:C_VALUE_DIM] / acc[C_VALUE_DIM:C_VALUE_DIM + 1]
        o_t = o2[:, :tq] - lam * o2[:, tq:]
        o_t = o_t * lax.rsqrt(jnp.mean(o_t * o_t, axis=0, keepdims=True) + NORM_EPS)
        o_ref[pl.ds(q0, tq), :] = (o_t.T * sub_gain * (1.0 - lam_init)).astype(BF16)
        return (base + (qi + 1) * n_diag) % 2

    lax.fori_loop(0, nq, q_block, jnp.int32(0))


def _attn(qk, vt, slopes, lam_params, sub_gain, *, bsz, seq, lam_init, tq):
    t = qk.shape[0]
    tk = vt.shape[3]
    assert tq % tk == 0
    kernel = functools.partial(_attn_kernel, tq=tq, tk=tk, lam_init=lam_init)
    return pl.pallas_call(
        kernel,
        out_shape=jax.ShapeDtypeStruct((t, C_HEADS * C_VALUE_DIM), BF16),
        grid_spec=pltpu.PrefetchScalarGridSpec(
            num_scalar_prefetch=1,
            grid=(bsz, C_HEADS),
            in_specs=[pl.BlockSpec((seq, LANES), lambda b, h, s: (b, h)),
                      pl.BlockSpec((seq, LANES), lambda b, h, s: (b, C_HEADS + h)),
                      pl.BlockSpec((None, seq // tk, V_ROWS, tk), lambda b, h, s: (b, 0, h, 0)),
                      pl.BlockSpec((4, C_HEAD_DIM), lambda b, h, s: (0, 0)),
                      pl.BlockSpec((1, C_VALUE_DIM), lambda b, h, s: (0, 0))],
            out_specs=pl.BlockSpec((seq, LANES), lambda b, h, s: (b, h)),
            scratch_shapes=[pltpu.VMEM((tk, 2 * tq), F32), pltpu.VMEM((tk, 2 * tq), F32),
                            pltpu.VMEM((SUBLANES, 2 * tq), F32), pltpu.VMEM((SUBLANES, 2 * tq), F32),
                            pltpu.VMEM((tk, 2 * tq), BF16),
                            pltpu.VMEM((2 * tq, LANES), BF16),
                            pltpu.VMEM((tk, 2 * tq), F32),
                            pltpu.VMEM((1, 2 * tq), F32),
                            pltpu.VMEM((V_ROWS, 2 * tq), F32)]),
        compiler_params=_params("parallel", "parallel"),
        name="diff_attn",
    )(slopes, qk, qk, vt, lam_params, sub_gain)


def _post_kernel(*refs, n_mix, layer, fc):
    h_ref = refs[0]
    mix_refs = refs[1:1 + n_mix]
    (w_out_ref, g_mlp_ref, w1_ref, w2_ref, g_ple_ref, w_gate_ref, w_proj_ref, p_ref, o_ref) = refs[1 + n_mix:]
    del layer
    mix = mix_refs[0][...] if n_mix == 1 else jnp.concatenate([m[...] for m in mix_refs], axis=1)
    h = h_ref[...] + _dot(mix, w_out_ref[...])
    hn = _rms(h, g_mlp_ref[...]).astype(BF16)
    acc = h
    for c in range(0, D_FF, fc):
        a = jnp.maximum(_dot(hn, w1_ref[:, c:c + fc]), 0.0)
        acc = acc + _dot((a * a).astype(BF16), w2_ref[c:c + fc, :])
    h = acc
    gate = jax.nn.sigmoid(_dot(_rms(h, g_ple_ref[...]).astype(BF16), w_gate_ref[...]))
    o_ref[...] = h + _dot(p_ref[...].astype(BF16), w_proj_ref[...]) * gate


def _post(h, mixes, w_out_all, i_mix, g_mlp, w1_all, w2_all, g_ple, w_gate_all, w_proj_all, p_all, layer,
          *, tm=512, fc=512):
    t = h.shape[0]
    row = lambda i: (i, 0)
    mix_specs = [pl.BlockSpec((tm, m.shape[1]), row) for m in mixes]
    return pl.pallas_call(
        functools.partial(_post_kernel, n_mix=len(mixes), layer=layer, fc=fc),
        out_shape=jax.ShapeDtypeStruct((t, D_MODEL), F32),
        grid=(t // tm,),
        in_specs=[pl.BlockSpec((tm, D_MODEL), row)] + mix_specs + [
            _stacked((D_MODEL, D_MODEL), i_mix), _resident((1, D_MODEL)),
            _stacked((D_MODEL, D_FF), layer), _stacked((D_FF, D_MODEL), layer),
            _resident((1, D_MODEL)), _stacked((D_MODEL, D_MODEL), layer),
            _stacked((PLE_DIM, D_MODEL), layer),
            pl.BlockSpec((None, tm, PLE_DIM), lambda i: (layer, i, 0))],
        out_specs=pl.BlockSpec((tm, D_MODEL), row),
        compiler_params=_params("parallel"),
        name="post",
    )(h, *mixes, w_out_all, g_mlp, w1_all, w2_all, g_ple, w_gate_all, w_proj_all, p_all)


def _lane_row(values, offset):
    return jnp.zeros((1, LANES), F32).at[0, offset:offset + values.shape[0]].set(values.astype(F32))


def kernel(x, p, ln_mix_e, w_in_e, gmlp_v_gain, gmlp_ws, gmlp_bs, gdn_conv, gdn_a_log, gdn_dt_bias,
           gdn_out_gain, w_out_e, ln_mix_o, w_qkv_o, attn_q_gain, attn_k_gain, diff_lambda,
           attn_sub_gain, w_out_o, ln_mlp, w_mlp1, w_mlp2, ln_ple, w_ple_gate, w_ple_proj):
    bsz, seq, d = x.shape
    depth = p.shape[0]
    t = bsz * seq
    h = x.reshape(t, d)
    p_all = p.reshape(depth, t, PLE_DIM)
    slopes = 2.0 ** (-8.0 * jnp.arange(1, C_HEADS + 1, dtype=F32) / C_HEADS)
    row = lambda a: a.reshape(1, -1).astype(F32)
    w_in_e16 = jnp.pad(w_in_e, ((0, 0), (0, 0), (0, EVEN_PAD - w_in_e.shape[2]))).astype(BF16)
    w_qkv_o16 = w_qkv_o.astype(BF16)
    w_out_e16, w_out_o16 = w_out_e.astype(BF16), w_out_o.astype(BF16)
    w_mlp1_16, w_mlp2_16 = w_mlp1.astype(BF16), w_mlp2.astype(BF16)
    w_gate16, w_proj16 = w_ple_gate.astype(BF16), w_ple_proj.astype(BF16)
    for layer in range(depth):
        i = layer // 2
        if layer % 2 == 0:
            a_out, proj = _in_even(h, row(ln_mix_e[i]), w_in_e16, row(gmlp_v_gain[i]), gmlp_ws,
                                   gmlp_bs[i].T, gdn_conv, i, seq=seq)
            b_out = _gdn(proj, _lane_row(gdn_a_log[i], B_HEADS), _lane_row(gdn_dt_bias[i], B_HEADS),
                         row(gdn_out_gain[i]), bsz=bsz, seq=seq)
            mixes = (a_out, b_out)
            w_out = w_out_e16
        else:
            lam_init = 0.8 - 0.6 * math.exp(-0.3 * layer)
            qg = row(jnp.concatenate([attn_q_gain[i], attn_q_gain[i]]))
            kg = row(jnp.concatenate([attn_k_gain[i], attn_k_gain[i]]))
            qk, vt = _in_odd(h, row(ln_mix_o[i]), w_qkv_o16, qg, kg, i, bsz=bsz, seq=seq, tk=ATTN_K_BLOCK)
            o = _attn(qk, vt, slopes, diff_lambda[i], row(attn_sub_gain[i]), bsz=bsz, seq=seq,
                      lam_init=lam_init, tq=ATTN_Q_BLOCK)
            mixes = (o,)
            w_out = w_out_o16
        h = _post(h, mixes, w_out, i, row(ln_mlp[layer]), w_mlp1_16, w_mlp2_16, row(ln_ple[layer]),
                  w_gate16, w_proj16, p_all, layer)
    return h.reshape(bsz, seq, d)
```

```python
import functools
import math

import jax
import jax.numpy as jnp
from jax import lax
from jax.experimental import pallas as pl
from jax.experimental.pallas import tpu as pltpu

F32 = jnp.float32
BF16 = jnp.bfloat16
NORM_EPS = 1e-6

LANES = 128
SUBLANES = 8
MXU_TILE = 256
D_MODEL = 1024
PLE_DIM = 256
D_FF = 4 * D_MODEL
A_WIDTH = 512
A_GROUPS = 4
A_CHUNK = 128
B_HEADS = 4
B_HEAD_DIM = 128
B_WIDTH = 512
B_CONV = 4
B_CHUNK = 64
CONV_PAD = 8
C_HEADS = 8
C_HEAD_DIM = 64
C_VALUE_DIM = 128
ATTN_Q_BLOCK = 512
ATTN_K_BLOCK = 512
ATTN_ROW_CHUNK = 32
V_ROWS = C_VALUE_DIM + 16
LOG2_E = math.log2(math.e)
EVEN_MAIN = 2 * A_WIDTH + 4 * B_WIDTH
EVEN_PAD = EVEN_MAIN + LANES
VMEM_LIMIT = 56 * 1024 * 1024


def _rms(x, gain):
    return x * lax.rsqrt(jnp.mean(x * x, axis=-1, keepdims=True) + NORM_EPS) * gain


def _dot(a, b):
    return jnp.dot(a, b, preferred_element_type=F32)


def _dot_nt(a, b):
    return lax.dot_general(a, b, (((1,), (1,)), ((), ())), preferred_element_type=F32)


def _resident(shape):
    zeros = (0,) * len(shape)
    return pl.BlockSpec(shape, lambda *_: zeros, pipeline_mode=pl.Buffered(1))


def _stacked(shape, index):
    zeros = (0,) * len(shape)
    return pl.BlockSpec((None,) + shape, lambda *_: (index,) + zeros, pipeline_mode=pl.Buffered(1))


def _params(*sem):
    return pltpu.CompilerParams(dimension_semantics=sem, vmem_limit_bytes=VMEM_LIMIT)


def _in_even_kernel(h_ref, g_ref, w_ref, vg_ref, ws_ref, bs_ref, cw_ref, a_ref, o_ref, buf_ref, *,
                    tm, tiles_per_seq):
    @pl.when(pl.program_id(0) % tiles_per_seq == 0)
    def _():
        buf_ref[:, 0:CONV_PAD, :] = jnp.zeros((3, CONV_PAD, B_WIDTH), F32)

    hn = _rms(h_ref[...], g_ref[...]).astype(BF16)
    u = jax.nn.gelu(_dot(hn, w_ref[:, 0:A_WIDTH]))
    v = jax.nn.gelu(_dot(hn, w_ref[:, A_WIDTH:2 * A_WIDTH]))
    for part in range(3):
        cols = slice(part * B_WIDTH, (part + 1) * B_WIDTH)
        x = _dot(hn, w_ref[:, 2 * A_WIDTH + cols.start:2 * A_WIDTH + cols.stop])
        buf_ref[part, CONV_PAD:CONV_PAD + tm, :] = x
        cw = cw_ref[:, cols]
        y = cw[B_CONV - 1:B_CONV, :] * x
        for j in range(B_CONV - 1):
            off = CONV_PAD - (B_CONV - 1) + j
            y = y + cw[j:j + 1, :] * buf_ref[part, off:off + tm, :]
        buf_ref[part, 0:CONV_PAD, :] = x[tm - CONV_PAD:tm, :]
        half = 0.5 * y
        y = half + half * jnp.tanh(half)
        if part == 2:
            o_ref[:, cols] = y
        else:
            for h in range(B_HEADS):
                hcols = slice(cols.start + h * LANES, cols.start + (h + 1) * LANES)
                yh = y[:, h * LANES:(h + 1) * LANES]
                yh = yh * lax.rsqrt(jnp.sum(yh * yh, axis=-1, keepdims=True) + NORM_EPS)
                o_ref[:, hcols] = yh * (B_HEAD_DIM ** -0.5) if part == 0 else yh
    rest = 2 * A_WIDTH + 3 * B_WIDTH
    o_ref[:, 3 * B_WIDTH:] = _dot(hn, w_ref[:, rest:])
    ii = lax.broadcasted_iota(jnp.int32, (A_CHUNK, A_CHUNK), 0)
    jj = lax.broadcasted_iota(jnp.int32, (A_CHUNK, A_CHUNK), 1)
    for g in range(A_GROUPS):
        cols = slice(g * LANES, (g + 1) * LANES)
        w = jnp.where(ii >= jj, ws_ref[g], 0.0).astype(BF16)
        bias = bs_ref[:, g:g + 1]
        gain = vg_ref[:, cols]
        for c in range(tm // A_CHUNK):
            rows = slice(c * A_CHUNK, (c + 1) * A_CHUNK)
            vg = v[rows, cols]
            vc = vg - jnp.mean(vg, axis=-1, keepdims=True)
            y = vc * lax.rsqrt(jnp.mean(vc * vc, axis=-1, keepdims=True) + NORM_EPS) * gain
            s = _dot(w, y.astype(BF16)) + bias
            a_ref[rows, cols] = (u[rows, cols] * s).astype(BF16)


def _in_even(h, gain, w_all, v_gain, ws_all, bs_t, conv_all, i, *, seq, tm=512):
    t = h.shape[0]
    return pl.pallas_call(
        functools.partial(_in_even_kernel, tm=tm, tiles_per_seq=seq // tm),
        out_shape=(jax.ShapeDtypeStruct((t, A_WIDTH), BF16),
                   jax.ShapeDtypeStruct((t, EVEN_PAD - 2 * A_WIDTH), F32)),
        grid=(t // tm,),
        in_specs=[pl.BlockSpec((tm, D_MODEL), lambda i_: (i_, 0)),
                  _resident((1, D_MODEL)), _stacked((D_MODEL, EVEN_PAD), i),
                  _resident((1, A_WIDTH)), _stacked((A_GROUPS, A_CHUNK, A_CHUNK), i),
                  _resident((A_CHUNK, A_GROUPS)), _stacked((B_CONV, 3 * B_WIDTH), i)],
        out_specs=(pl.BlockSpec((tm, A_WIDTH), lambda i_: (i_, 0)),
                   pl.BlockSpec((tm, EVEN_PAD - 2 * A_WIDTH), lambda i_: (i_, 0))),
        scratch_shapes=[pltpu.VMEM((3, CONV_PAD + tm, B_WIDTH), F32)],
        compiler_params=_params("arbitrary"),
        name="in_even",
    )(h, gain, w_all, v_gain, ws_all, bs_t, conv_all)


def _unit_lower_inverses(a_list, eye, level_masks):
    ds = [eye - jnp.where(level_masks[0], a, 0.0) for a in a_list]
    for mask in level_masks[1:]:
        d16 = [d.astype(BF16) for d in ds]
        mds = [_dot(jnp.where(mask, a, 0.0).astype(BF16), d) for a, d in zip(a_list, d16)]
        ds = [d - _dot(d_lo, md.astype(BF16)) for d, d_lo, md in zip(ds, d16, mds)]
    return ds


def _gdn_kernel(qn_ref, kn_ref, vn_ref, z_ref, ba_ref, alog_ref, dtb_ref, og_ref, o_ref, state_ref,
                lhs_ref, ds_ref, o0_ref, gl_ref, raw_ref, *, tb, group):
    c_len = B_CHUNK

    @pl.when(pl.program_id(1) == 0)
    def _():
        state_ref[...] = jnp.zeros_like(state_ref)

    ii = lax.broadcasted_iota(jnp.int32, (c_len, c_len), 0)
    jj = lax.broadcasted_iota(jnp.int32, (c_len, c_len), 1)
    causal = ii >= jj
    strict = ii > jj
    eye = jnp.where(ii == jj, 1.0, 0.0).astype(F32)
    ltri = jnp.where(causal, 1.0, 0.0).astype(BF16)
    level_masks = []
    s = 1
    while s < c_len:
        level_masks.append(((ii // (2 * s)) == (jj // (2 * s))) & ((ii % (2 * s)) >= s) & ((jj % (2 * s)) < s))
        s *= 2
    neg_rate = -jnp.exp(alog_ref[...])
    dt_bias = dtb_ref[...]
    out_gain = og_ref[...]

    heads = range(B_HEADS)

    def prepare(gi, carry):
        prob = []
        for cc in range(group):
            c = gi * group + cc
            rows = pl.ds(pl.multiple_of(c * c_len, c_len), c_len)
            ba = ba_ref[rows, :]
            beta = jax.nn.sigmoid(ba)
            g = neg_rate * jax.nn.softplus(ba + dt_bias)
            g_hi = g.astype(BF16)
            g_lo = (g - g_hi.astype(F32)).astype(BF16)
            gc = _dot(ltri, g_hi) + _dot(ltri, g_lo)
            gc_t = gc.T
            for h in heads:
                cols = slice(h * LANES, (h + 1) * LANES)
                q_h = qn_ref[rows, cols]
                k_h = kn_ref[rows, cols]
                gcol = gc[:, B_HEADS + h:B_HEADS + h + 1]
                grow = gc_t[B_HEADS + h:B_HEADS + h + 1, :]
                glast = gc[c_len - 1:c_len, B_HEADS + h:B_HEADS + h + 1]
                bcol = beta[:, h:h + 1]
                eg = jnp.exp(gcol)
                k_beta = k_h * bcol
                prob.append(dict(
                    c=c, h=h, q_dec=q_h * eg,
                    decay=jnp.where(causal, jnp.exp(jnp.where(causal, gcol - grow, 0.0)), 0.0),
                    lhs=jnp.concatenate([q_h, k_beta], axis=0).astype(BF16), k16=k_h.astype(BF16),
                    rhs=jnp.concatenate([k_beta * eg, vn_ref[rows, cols] * bcol], axis=1).astype(BF16),
                    k_dec=k_h * jnp.exp(glast - gcol), gl=jnp.exp(glast)))
        kks = [_dot_nt(p["lhs"], p["k16"]) for p in prob]
        a_list = [jnp.where(strict, kk[c_len:] * p["decay"], 0.0) for kk, p in zip(kks, prob)]
        t_invs = _unit_lower_inverses(a_list, eye, level_masks)
        wus = [_dot(t.astype(BF16), p["rhs"]).astype(BF16) for t, p in zip(t_invs, prob)]
        kd_wu = [_dot(p["k_dec"].T.astype(BF16), wu) for p, wu in zip(prob, wus)]
        qk_wu = [_dot((kk[:c_len] * p["decay"]).astype(BF16), wu) for p, kk, wu in zip(prob, kks, wus)]
        for p, kd, qk in zip(prob, kd_wu, qk_wu):
            c, h = p["c"], p["h"]
            lhs_ref[c, h] = jnp.concatenate(
                [kd[:, :B_HEAD_DIM], qk[:, :B_HEAD_DIM] - p["q_dec"]], axis=0).astype(BF16)
            ds_ref[c, h] = kd[:, B_HEAD_DIM:]
            o0_ref[c, h] = qk[:, B_HEAD_DIM:]
            gl_ref[c, h] = jnp.broadcast_to(p["gl"], (1, LANES))
        return carry

    lax.fori_loop(0, tb // (c_len * group), prepare, 0)

    def recur(c, carry):
        rows = pl.ds(pl.multiple_of(c * c_len, c_len), c_len)
        states = [state_ref[h] for h in heads]
        prods = [_dot(lhs_ref[c, h], states[h].astype(BF16)) for h in heads]
        for h in heads:
            state_ref[h] = states[h] * gl_ref[c, h] + (ds_ref[c, h] - prods[h][:B_HEAD_DIM])
            raw_ref[rows, h * LANES:(h + 1) * LANES] = o0_ref[c, h] - prods[h][B_HEAD_DIM:]
        return carry

    lax.fori_loop(0, tb // c_len, recur, 0)

    for h in heads:
        cols = slice(h * LANES, (h + 1) * LANES)
        z = z_ref[:, cols]
        o_ref[:, cols] = (_rms(raw_ref[:, cols], out_gain) * (z * jax.nn.sigmoid(z))).astype(BF16)


def _gdn(proj, alog_row, dtb_row, out_gain, *, bsz, seq, tb=512, group=8):
    t = proj.shape[0]
    nt = seq // tb

    def col(j):
        return pl.BlockSpec((tb, B_WIDTH), lambda b, i: (b * nt + i, j))

    nc = tb // B_CHUNK
    per_problem = lambda *shape_dtype: pltpu.VMEM((nc, B_HEADS) + shape_dtype[:-1], shape_dtype[-1])
    return pl.pallas_call(
        functools.partial(_gdn_kernel, tb=tb, group=group),
        out_shape=jax.ShapeDtypeStruct((t, B_WIDTH), BF16),
        grid=(bsz, nt),
        in_specs=[col(0), col(1), col(2), col(3),
                  pl.BlockSpec((tb, LANES), lambda b, i: (b * nt + i, 4 * B_WIDTH // LANES)),
                  pl.BlockSpec((1, LANES), lambda b, i: (0, 0)),
                  pl.BlockSpec((1, LANES), lambda b, i: (0, 0)),
                  pl.BlockSpec((1, B_HEAD_DIM), lambda b, i: (0, 0))],
        out_specs=pl.BlockSpec((tb, B_WIDTH), lambda b, i: (b * nt + i, 0)),
        scratch_shapes=[pltpu.VMEM((B_HEADS, B_HEAD_DIM, B_HEAD_DIM), F32),
                        per_problem(B_HEAD_DIM + B_CHUNK, B_HEAD_DIM, BF16),
                        per_problem(B_HEAD_DIM, B_HEAD_DIM, F32),
                        per_problem(B_CHUNK, B_HEAD_DIM, F32),
                        per_problem(1, LANES, F32),
                        pltpu.VMEM((tb, B_WIDTH), F32)],
        compiler_params=_params("parallel", "arbitrary"),
        name="gdn",
    )(proj, proj, proj, proj, proj, alog_row, dtb_row, out_gain)


def _in_odd_kernel(h_ref, g_ref, w_ref, qg_ref, kg_ref, o_ref, vt_ref, *, tm, tn, tk):
    hn = _rms(h_ref[...], g_ref[...]).astype(BF16)
    lane = lax.broadcasted_iota(jnp.int32, (tm, LANES), 1)
    low = lane < C_HEAD_DIM
    qk_width = 2 * C_HEADS * 2 * C_HEAD_DIM
    for start in range(0, 3 * C_HEADS * C_VALUE_DIM, tn):
        y = _dot(hn, w_ref[:, start:start + tn])
        if start >= qk_width:
            extra = jnp.where(lax.broadcasted_iota(jnp.int32, (V_ROWS - C_VALUE_DIM, tk), 0) == 0, 1.0, 0.0)
            for hh in range(tn // C_VALUE_DIM):
                row0 = ((start - qk_width) // C_VALUE_DIM + hh) * V_ROWS
                for r0 in range(0, tm, MXU_TILE):
                    vt_ref[r0 // tk, row0:row0 + C_VALUE_DIM, r0 % tk:r0 % tk + MXU_TILE] = (
                        y[r0:r0 + MXU_TILE, hh * C_VALUE_DIM:(hh + 1) * C_VALUE_DIM].T.astype(BF16))
                for kb in range(tm // tk):
                    vt_ref[kb, row0 + C_VALUE_DIM:row0 + V_ROWS, :] = extra.astype(BF16)
            continue
        is_q = start < qk_width // 2
        gain = qg_ref[...] if is_q else kg_ref[...]
        for j in range(tn // LANES):
            yb = y[:, j * LANES:(j + 1) * LANES]
            y2 = yb * yb
            lo = jnp.sum(jnp.where(low, y2, 0.0), axis=-1, keepdims=True)
            hi = jnp.sum(jnp.where(low, 0.0, y2), axis=-1, keepdims=True)
            ms = jnp.where(low, lo, hi) * (1.0 / C_HEAD_DIM)
            yn = yb * lax.rsqrt(ms + NORM_EPS) * gain
            if is_q:
                yn = yn * (C_HEAD_DIM ** -0.5 * LOG2_E)
            o_ref[:, start + j * LANES:start + (j + 1) * LANES] = yn.astype(BF16)


def _in_odd(h, gain, w_all, qg, kg, i_layer, *, bsz, seq, tk, tm=512, tn=512):
    t = h.shape[0]
    n = w_all.shape[2]
    nt = seq // tm
    qk_width = 2 * C_HEADS * 2 * C_HEAD_DIM
    return pl.pallas_call(
        functools.partial(_in_odd_kernel, tm=tm, tn=tn, tk=tk),
        out_shape=(jax.ShapeDtypeStruct((t, qk_width), BF16),
                   jax.ShapeDtypeStruct((bsz, seq // tk, C_HEADS * V_ROWS, tk), BF16)),
        grid=(t // tm,),
        in_specs=[pl.BlockSpec((tm, D_MODEL), lambda i: (i, 0)),
                  _resident((1, D_MODEL)), _stacked((D_MODEL, n), i_layer),
                  _resident((1, LANES)), _resident((1, LANES))],
        out_specs=(pl.BlockSpec((tm, qk_width), lambda i: (i, 0)),
                   pl.BlockSpec((None, tm // tk, C_HEADS * V_ROWS, tk), lambda i: (i // nt, i % nt, 0, 0))),
        compiler_params=_params("parallel"),
        name="in_odd",
    )(h, gain, w_all, qg, kg)


def _attn_kernel(slopes_ref, q_ref, k_ref, vt_ref, lam_ref, sg_ref, o_ref, s0_ref, s1_ref, mx0_ref,
                 mx1_ref, p_ref, q2_ref, rel_ref, m_ref, acc_ref, *, tq, tk, lam_init):
    slope = slopes_ref[pl.program_id(1)] * LOG2_E
    nq = q_ref.shape[0] // tq
    ki = lax.broadcasted_iota(jnp.int32, (tk, 2 * tq), 0)
    qj = lax.broadcasted_iota(jnp.int32, (tk, 2 * tq), 1)
    qj = jnp.where(qj >= tq, qj - tq, qj)
    rel_ref[...] = (ki - qj).astype(F32) * slope
    lane = lax.broadcasted_iota(jnp.int32, (tq, LANES), 1)
    lp = lam_ref[...]
    lam = (jnp.exp(jnp.sum(lp[0:1] * lp[1:2], axis=-1, keepdims=True))
           - jnp.exp(jnp.sum(lp[2:3] * lp[3:4], axis=-1, keepdims=True)) + lam_init)
    sub_gain = sg_ref[...]

    def load_q2(qi):
        q = q_ref[pl.ds(pl.multiple_of(qi * tq, tq), tq), :]
        zero = jnp.zeros_like(q)
        return jnp.concatenate([jnp.where(lane < C_HEAD_DIM, q, zero),
                                jnp.where(lane < C_HEAD_DIM, zero, q)], axis=0)

    lanes2 = 2 * tq
    n_tile = lanes2 // MXU_TILE
    n_slab = tk // MXU_TILE
    n_diag = tq // tk
    s_bufs = (s0_ref, s1_ref)
    mx_bufs = (mx0_ref, mx1_ref)
    chunk = ATTN_ROW_CHUNK

    def fold_max(x):
        return jnp.max(x.reshape(x.shape[0] // SUBLANES, SUBLANES, x.shape[1]), axis=0)

    def exp_rows(shift, par, rows, cols):
        p_ref[rows, cols] = jnp.exp2((s_bufs[par][rows, cols] + shift[:, cols]).astype(BF16))

    def produce_scores(par, k_blk, q2, j):
        cols = slice(j * MXU_TILE, (j + 1) * MXU_TILE)
        t = _dot_nt(k_blk, q2[cols, :]) + rel_ref[:, cols]
        s_bufs[par][:, cols] = t
        mx_bufs[par][:, cols] = fold_max(t)

    q2_first = load_q2(0)
    q2_ref[...] = q2_first
    for j in range(n_tile):
        produce_scores(0, k_ref[0:tk, :], q2_first, j)

    def q_block(qi, base):
        q0 = pl.multiple_of(qi * tq, tq)
        q2 = q2_ref[...]
        m_ref[...] = jnp.full(m_ref.shape, -jnp.inf, F32)
        acc_ref[...] = jnp.zeros(acc_ref.shape, F32)

        def step(kb, diag, par):
            if diag == n_diag - 1:
                q2_next = load_q2(jnp.minimum(qi + 1, nq - 1))
                q2_ref[...] = q2_next
                k_next = k_ref[0:tk, :]
            else:
                q2_next = q2
                k_next = k_ref[pl.ds(pl.multiple_of((kb + 1) * tk, tk), tk), :]

            def scores_piece(j):
                produce_scores(1 - par, k_next, q2_next, j)

            def lead(j):
                return (j * MXU_TILE) % tq - diag * tk

            def live_slabs(j):
                if diag is None:
                    return n_slab
                return min(max(lead(j) // MXU_TILE + 1, 0), n_slab)

            scores_piece(0)
            if diag is None:
                mx = mx_bufs[par][...]
            else:
                mx_tiles = []
                for j in range(n_tile):
                    cols = slice(j * MXU_TILE, (j + 1) * MXU_TILE)
                    mx_j = jnp.full((SUBLANES, MXU_TILE), -jnp.inf, F32)
                    for r0 in range(0, live_slabs(j) * MXU_TILE, chunk):
                        t = s_bufs[par][r0:r0 + chunk, cols]
                        if r0 // MXU_TILE == lead(j) // MXU_TILE:
                            krow = lax.broadcasted_iota(jnp.int32, (chunk, MXU_TILE), 0) + r0
                            qcol = lax.broadcasted_iota(jnp.int32, (chunk, MXU_TILE), 1) + lead(j)
                            t = jnp.where(krow <= qcol, t, -jnp.inf)
                            s_bufs[par][r0:r0 + chunk, cols] = t
                        mx_j = jnp.maximum(mx_j, fold_max(t))
                    mx_tiles.append(mx_j)
                mx = jnp.concatenate(mx_tiles, axis=1)
            offset = slope * (kb * tk - q0).astype(F32)
            m_prev = m_ref[...]
            m_new = jnp.maximum(m_prev, jnp.max(mx, axis=0, keepdims=True) + offset)
            alpha = jnp.exp2(m_prev - m_new)
            shift = offset - m_new
            pv = [None] * n_tile
            for slab in range(n_slab):
                keys = slice(slab * MXU_TILE, (slab + 1) * MXU_TILE)
                tiles = [j for j in range(n_tile) if slab < live_slabs(j)]
                for r0 in range(keys.start, keys.stop, chunk):
                    for j in tiles:
                        exp_rows(shift, par, slice(r0, r0 + chunk), slice(j * MXU_TILE, (j + 1) * MXU_TILE))
                first = 1 + slab * (n_tile - 1) // n_slab
                for j in range(first, 1 + (slab + 1) * (n_tile - 1) // n_slab):
                    scores_piece(j)
                for j in tiles:
                    part = _dot(vt_ref[kb][:, keys], p_ref[keys, j * MXU_TILE:(j + 1) * MXU_TILE])
                    pv[j] = part if pv[j] is None else pv[j] + part
            dead = jnp.zeros((V_ROWS, MXU_TILE), F32)
            pv = [dead if part is None else part for part in pv]
            acc_ref[...] = alpha * acc_ref[...] + jnp.concatenate(pv, axis=1)
            m_ref[...] = m_new

        def either_parity(kb, diag):
            dyn_par = (base + kb) % 2
            for par in range(2):
                pl.when(dyn_par == par)(functools.partial(step, kb, diag, par))

        def body(kb, c):
            either_parity(kb, None)
            return c

        lax.fori_loop(0, qi * n_diag, body, 0)
        for diag in range(n_diag):
            either_parity(qi * n_diag + diag, diag)
        acc = acc_ref[...]
        o2 = acc[:C_VALUE_DIM] / acc[C_VALUE_DIM:C_VALUE_DIM + 1]
        o_t = o2[:, :tq] - lam * o2[:, tq:]
        o_t = o_t * lax.rsqrt(jnp.mean(o_t * o_t, axis=0, keepdims=True) + NORM_EPS)
        o_ref[pl.ds(q0, tq), :] = (o_t.T * sub_gain * (1.0 - lam_init)).astype(BF16)
        return (base + (qi + 1) * n_diag) % 2

    lax.fori_loop(0, nq, q_block, jnp.int32(0))


def _attn(qk, vt, slopes, lam_params, sub_gain, *, bsz, seq, lam_init, tq):
    t = qk.shape[0]
    tk = vt.shape[3]
    assert tq % tk == 0
    kernel = functools.partial(_attn_kernel, tq=tq, tk=tk, lam_init=lam_init)
    return pl.pallas_call(
        kernel,
        out_shape=jax.ShapeDtypeStruct((t, C_HEADS * C_VALUE_DIM), BF16),
        grid_spec=pltpu.PrefetchScalarGridSpec(
            num_scalar_prefetch=1,
            grid=(bsz, C_HEADS),
            in_specs=[pl.BlockSpec((seq, LANES), lambda b, h, s: (b, h)),
                      pl.BlockSpec((seq, LANES), lambda b, h, s: (b, C_HEADS + h)),
                      pl.BlockSpec((None, seq // tk, V_ROWS, tk), lambda b, h, s: (b, 0, h, 0)),
                      pl.BlockSpec((4, C_HEAD_DIM), lambda b, h, s: (0, 0)),
                      pl.BlockSpec((1, C_VALUE_DIM), lambda b, h, s: (0, 0))],
            out_specs=pl.BlockSpec((seq, LANES), lambda b, h, s: (b, h)),
            scratch_shapes=[pltpu.VMEM((tk, 2 * tq), F32), pltpu.VMEM((tk, 2 * tq), F32),
                            pltpu.VMEM((SUBLANES, 2 * tq), F32), pltpu.VMEM((SUBLANES, 2 * tq), F32),
                            pltpu.VMEM((tk, 2 * tq), BF16),
                            pltpu.VMEM((2 * tq, LANES), BF16),
                            pltpu.VMEM((tk, 2 * tq), F32),
                            pltpu.VMEM((1, 2 * tq), F32),
                            pltpu.VMEM((V_ROWS, 2 * tq), F32)]),
        compiler_params=_params("parallel", "parallel"),
        name="diff_attn",
    )(slopes, qk, qk, vt, lam_params, sub_gain)


def _post_kernel(*refs, n_mix, convert_next, fc):
    h_ref = refs[0]
    mix_refs = refs[1:1 + n_mix]
    (w_out_ref, g_mlp_ref, w1_ref, w2_ref, g_ple_ref, w_gate_ref, w_proj_ref, p_ref) = refs[1 + n_mix:9 + n_mix]
    if convert_next:
        w1_f32_ref, w2_f32_ref, o_ref, w1_next_ref, w2_next_ref = refs[9 + n_mix:]
        w1_next_ref[...] = w1_f32_ref[...].astype(BF16)
        w2_next_ref[...] = w2_f32_ref[...].astype(BF16)
    else:
        (o_ref,) = refs[9 + n_mix:]
    mix = mix_refs[0][...] if n_mix == 1 else jnp.concatenate([m[...] for m in mix_refs], axis=1)
    h = h_ref[...] + _dot(mix, w_out_ref[...])
    hn = _rms(h, g_mlp_ref[...]).astype(BF16)
    acc = h
    for c in range(0, D_FF, fc):
        a = jnp.maximum(_dot(hn, w1_ref[:, c:c + fc]), 0.0)
        acc = acc + _dot((a * a).astype(BF16), w2_ref[c:c + fc, :])
    h = acc
    gate = jax.nn.sigmoid(_dot(_rms(h, g_ple_ref[...]).astype(BF16), w_gate_ref[...]))
    o_ref[...] = h + _dot(p_ref[...].astype(BF16), w_proj_ref[...]) * gate


def _post(h, mixes, w_out_all, i_mix, g_mlp, w1, w2, g_ple, w_gate_all, w_proj_all, p_all, layer,
          next_mlp_f32=None, *, tm=512, fc=512):
    t = h.shape[0]
    steps = t // tm
    row = lambda i: (i, 0)
    mix_specs = [pl.BlockSpec((tm, m.shape[1]), row) for m in mixes]
    in_specs = [pl.BlockSpec((tm, D_MODEL), row)] + mix_specs + [
        _stacked((D_MODEL, D_MODEL), i_mix), _resident((1, D_MODEL)),
        _resident((D_MODEL, D_FF)), _resident((D_FF, D_MODEL)),
        _resident((1, D_MODEL)), _stacked((D_MODEL, D_MODEL), layer),
        _stacked((PLE_DIM, D_MODEL), layer),
        pl.BlockSpec((None, tm, PLE_DIM), lambda i: (layer, i, 0))]
    out_shape = [jax.ShapeDtypeStruct((t, D_MODEL), F32)]
    out_specs = [pl.BlockSpec((tm, D_MODEL), row)]
    args = [h, *mixes, w_out_all, g_mlp, w1, w2, g_ple, w_gate_all, w_proj_all, p_all]
    if next_mlp_f32 is not None:
        for w_all in next_mlp_f32:
            rows, cols = w_all.shape[1] // steps, w_all.shape[2]
            in_specs.append(pl.BlockSpec((None, rows, cols), lambda i: (layer + 1, i, 0)))
            out_shape.append(jax.ShapeDtypeStruct(w_all.shape[1:], BF16))
            out_specs.append(pl.BlockSpec((rows, cols), row))
            args.append(w_all)
    out = pl.pallas_call(
        functools.partial(_post_kernel, n_mix=len(mixes), convert_next=next_mlp_f32 is not None, fc=fc),
        out_shape=out_shape,
        grid=(steps,),
        in_specs=in_specs,
        out_specs=out_specs,
        compiler_params=_params("parallel"),
        name="post",
    )(*args)
    return out[0] if next_mlp_f32 is None else out


def _lane_row(values, offset):
    return jnp.zeros((1, LANES), F32).at[0, offset:offset + values.shape[0]].set(values.astype(F32))


def kernel(x, p, ln_mix_e, w_in_e, gmlp_v_gain, gmlp_ws, gmlp_bs, gdn_conv, gdn_a_log, gdn_dt_bias,
           gdn_out_gain, w_out_e, ln_mix_o, w_qkv_o, attn_q_gain, attn_k_gain, diff_lambda,
           attn_sub_gain, w_out_o, ln_mlp, w_mlp1, w_mlp2, ln_ple, w_ple_gate, w_ple_proj):
    bsz, seq, d = x.shape
    depth = p.shape[0]
    t = bsz * seq
    h = x.reshape(t, d)
    p_all = p.reshape(depth, t, PLE_DIM)
    slopes = 2.0 ** (-8.0 * jnp.arange(1, C_HEADS + 1, dtype=F32) / C_HEADS)
    row = lambda a: a.reshape(1, -1).astype(F32)
    w_in_e16 = jnp.pad(w_in_e, ((0, 0), (0, 0), (0, EVEN_PAD - w_in_e.shape[2]))).astype(BF16)
    w_qkv_o16 = w_qkv_o.astype(BF16)
    w_out_e16, w_out_o16 = w_out_e.astype(BF16), w_out_o.astype(BF16)
    w1, w2 = w_mlp1[0].astype(BF16), w_mlp2[0].astype(BF16)
    w_gate16, w_proj16 = w_ple_gate.astype(BF16), w_ple_proj.astype(BF16)
    for layer in range(depth):
        i = layer // 2
        if layer % 2 == 0:
            a_out, proj = _in_even(h, row(ln_mix_e[i]), w_in_e16, row(gmlp_v_gain[i]), gmlp_ws,
                                   gmlp_bs[i].T, gdn_conv, i, seq=seq)
            b_out = _gdn(proj, _lane_row(gdn_a_log[i], B_HEADS), _lane_row(gdn_dt_bias[i], B_HEADS),
                         row(gdn_out_gain[i]), bsz=bsz, seq=seq)
            mixes = (a_out, b_out)
            w_out = w_out_e16
        else:
            lam_init = 0.8 - 0.6 * math.exp(-0.3 * layer)
            qg = row(jnp.concatenate([attn_q_gain[i], attn_q_gain[i]]))
            kg = row(jnp.concatenate([attn_k_gain[i], attn_k_gain[i]]))
            qk, vt = _in_odd(h, row(ln_mix_o[i]), w_qkv_o16, qg, kg, i, bsz=bsz, seq=seq, tk=ATTN_K_BLOCK)
            o = _attn(qk, vt, slopes, diff_lambda[i], row(attn_sub_gain[i]), bsz=bsz, seq=seq,
                      lam_init=lam_init, tq=ATTN_Q_BLOCK)
            mixes = (o,)
            w_out = w_out_o16
        last = layer + 1 == depth
        out = _post(h, mixes, w_out, i, row(ln_mlp[layer]), w1, w2, row(ln_ple[layer]), w_gate16, w_proj16,
                    p_all, layer, None if last else (w_mlp1, w_mlp2))
        h, w1, w2 = (out, None, None) if last else out
    return h.reshape(bsz, seq, d)
```

```python
import functools
import math

import jax
import jax.numpy as jnp
from jax import lax
from jax.experimental import pallas as pl
from jax.experimental.pallas import tpu as pltpu

F32 = jnp.float32
BF16 = jnp.bfloat16
NORM_EPS = 1e-6

LANES = 128
SUBLANES = 8
MXU_TILE = 256
D_MODEL = 1024
PLE_DIM = 256
D_FF = 4 * D_MODEL
A_WIDTH = 512
A_GROUPS = 4
A_CHUNK = 128
B_HEADS = 4
B_HEAD_DIM = 128
B_WIDTH = 512
B_CONV = 4
B_CHUNK = 64
CONV_PAD = 8
C_HEADS = 8
C_HEAD_DIM = 64
C_VALUE_DIM = 128
ATTN_Q_BLOCK = 512
ATTN_K_BLOCK = 512
ATTN_ROW_CHUNK = 32
V_ROWS = C_VALUE_DIM + 16
LOG2_E = math.log2(math.e)
EVEN_MAIN = 2 * A_WIDTH + 4 * B_WIDTH
EVEN_PAD = EVEN_MAIN + LANES
VMEM_LIMIT = 56 * 1024 * 1024


def _rms(x, gain):
    return x * lax.rsqrt(jnp.mean(x * x, axis=-1, keepdims=True) + NORM_EPS) * gain


def _dot(a, b):
    return jnp.dot(a, b, preferred_element_type=F32)


def _dot_nt(a, b):
    return lax.dot_general(a, b, (((1,), (1,)), ((), ())), preferred_element_type=F32)


def _resident(shape):
    zeros = (0,) * len(shape)
    return pl.BlockSpec(shape, lambda *_: zeros, pipeline_mode=pl.Buffered(1))


def _stacked(shape, index):
    zeros = (0,) * len(shape)
    return pl.BlockSpec((None,) + shape, lambda *_: (index,) + zeros, pipeline_mode=pl.Buffered(1))


def _params(*sem):
    return pltpu.CompilerParams(dimension_semantics=sem, vmem_limit_bytes=VMEM_LIMIT)


def _in_even_kernel(h_ref, g_ref, w_ref, vg_ref, ws_ref, bs_ref, cw_ref, a_ref, o_ref, buf_ref, *,
                    tm, tiles_per_seq):
    @pl.when(pl.program_id(0) % tiles_per_seq == 0)
    def _():
        buf_ref[:, 0:CONV_PAD, :] = jnp.zeros((3, CONV_PAD, B_WIDTH), F32)

    hn = _rms(h_ref[...], g_ref[...]).astype(BF16)
    u = jax.nn.gelu(_dot(hn, w_ref[:, 0:A_WIDTH]))
    v = jax.nn.gelu(_dot(hn, w_ref[:, A_WIDTH:2 * A_WIDTH]))
    for part in range(3):
        cols = slice(part * B_WIDTH, (part + 1) * B_WIDTH)
        x = _dot(hn, w_ref[:, 2 * A_WIDTH + cols.start:2 * A_WIDTH + cols.stop])
        buf_ref[part, CONV_PAD:CONV_PAD + tm, :] = x
        cw = cw_ref[:, cols]
        y = cw[B_CONV - 1:B_CONV, :] * x
        for j in range(B_CONV - 1):
            off = CONV_PAD - (B_CONV - 1) + j
            y = y + cw[j:j + 1, :] * buf_ref[part, off:off + tm, :]
        buf_ref[part, 0:CONV_PAD, :] = x[tm - CONV_PAD:tm, :]
        half = 0.5 * y
        y = half + half * jnp.tanh(half)
        if part == 2:
            o_ref[:, cols] = y
        else:
            for h in range(B_HEADS):
                hcols = slice(cols.start + h * LANES, cols.start + (h + 1) * LANES)
                yh = y[:, h * LANES:(h + 1) * LANES]
                yh = yh * lax.rsqrt(jnp.sum(yh * yh, axis=-1, keepdims=True) + NORM_EPS)
                o_ref[:, hcols] = yh * (B_HEAD_DIM ** -0.5) if part == 0 else yh
    rest = 2 * A_WIDTH + 3 * B_WIDTH
    o_ref[:, 3 * B_WIDTH:] = _dot(hn, w_ref[:, rest:])
    ii = lax.broadcasted_iota(jnp.int32, (A_CHUNK, A_CHUNK), 0)
    jj = lax.broadcasted_iota(jnp.int32, (A_CHUNK, A_CHUNK), 1)
    for g in range(A_GROUPS):
        cols = slice(g * LANES, (g + 1) * LANES)
        w = jnp.where(ii >= jj, ws_ref[g], 0.0).astype(BF16)
        bias = bs_ref[:, g:g + 1]
        gain = vg_ref[:, cols]
        for c in range(tm // A_CHUNK):
            rows = slice(c * A_CHUNK, (c + 1) * A_CHUNK)
            vg = v[rows, cols]
            vc = vg - jnp.mean(vg, axis=-1, keepdims=True)
            y = vc * lax.rsqrt(jnp.mean(vc * vc, axis=-1, keepdims=True) + NORM_EPS) * gain
            s = _dot(w, y.astype(BF16)) + bias
            a_ref[rows, cols] = (u[rows, cols] * s).astype(BF16)


def _in_even(h, gain, w_all, v_gain, ws_all, bs_t, conv_all, i, *, seq, tm=512):
    t = h.shape[0]
    return pl.pallas_call(
        functools.partial(_in_even_kernel, tm=tm, tiles_per_seq=seq // tm),
        out_shape=(jax.ShapeDtypeStruct((t, A_WIDTH), BF16),
                   jax.ShapeDtypeStruct((t, EVEN_PAD - 2 * A_WIDTH), F32)),
        grid=(t // tm,),
        in_specs=[pl.BlockSpec((tm, D_MODEL), lambda i_: (i_, 0)),
                  _resident((1, D_MODEL)), _stacked((D_MODEL, EVEN_PAD), i),
                  _resident((1, A_WIDTH)), _stacked((A_GROUPS, A_CHUNK, A_CHUNK), i),
                  _resident((A_CHUNK, A_GROUPS)), _stacked((B_CONV, 3 * B_WIDTH), i)],
        out_specs=(pl.BlockSpec((tm, A_WIDTH), lambda i_: (i_, 0)),
                   pl.BlockSpec((tm, EVEN_PAD - 2 * A_WIDTH), lambda i_: (i_, 0))),
        scratch_shapes=[pltpu.VMEM((3, CONV_PAD + tm, B_WIDTH), F32)],
        compiler_params=_params("arbitrary"),
        name="in_even",
    )(h, gain, w_all, v_gain, ws_all, bs_t, conv_all)


def _unit_lower_inverses(a_list, eye, level_masks):
    ds = [eye - jnp.where(level_masks[0], a, 0.0) for a in a_list]
    for mask in level_masks[1:]:
        d16 = [d.astype(BF16) for d in ds]
        mds = [_dot(jnp.where(mask, a, 0.0).astype(BF16), d) for a, d in zip(a_list, d16)]
        ds = [d - _dot(d_lo, md.astype(BF16)) for d, d_lo, md in zip(ds, d16, mds)]
    return ds


def _gdn_kernel(qn_ref, kn_ref, vn_ref, z_ref, ba_ref, alog_ref, dtb_ref, og_ref, o_ref, state_ref,
                lhs_ref, ds_ref, o0_ref, gl_ref, raw_ref, *, tb, group):
    c_len = B_CHUNK

    @pl.when(pl.program_id(1) == 0)
    def _():
        state_ref[...] = jnp.zeros_like(state_ref)

    ii = lax.broadcasted_iota(jnp.int32, (c_len, c_len), 0)
    jj = lax.broadcasted_iota(jnp.int32, (c_len, c_len), 1)
    causal = ii >= jj
    strict = ii > jj
    eye = jnp.where(ii == jj, 1.0, 0.0).astype(F32)
    ltri = jnp.where(causal, 1.0, 0.0).astype(BF16)
    level_masks = []
    s = 1
    while s < c_len:
        level_masks.append(((ii // (2 * s)) == (jj // (2 * s))) & ((ii % (2 * s)) >= s) & ((jj % (2 * s)) < s))
        s *= 2
    neg_rate = -jnp.exp(alog_ref[...])
    dt_bias = dtb_ref[...]
    out_gain = og_ref[...]

    heads = range(B_HEADS)

    def prepare(gi, carry):
        prob = []
        for cc in range(group):
            c = gi * group + cc
            rows = pl.ds(pl.multiple_of(c * c_len, c_len), c_len)
            ba = ba_ref[rows, :]
            beta = jax.nn.sigmoid(ba)
            g = neg_rate * jax.nn.softplus(ba + dt_bias)
            g_hi = g.astype(BF16)
            g_lo = (g - g_hi.astype(F32)).astype(BF16)
            gc = _dot(ltri, g_hi) + _dot(ltri, g_lo)
            gc_t = gc.T
            for h in heads:
                cols = slice(h * LANES, (h + 1) * LANES)
                q_h = qn_ref[rows, cols]
                k_h = kn_ref[rows, cols]
                gcol = gc[:, B_HEADS + h:B_HEADS + h + 1]
                grow = gc_t[B_HEADS + h:B_HEADS + h + 1, :]
                glast = gc[c_len - 1:c_len, B_HEADS + h:B_HEADS + h + 1]
                bcol = beta[:, h:h + 1]
                eg = jnp.exp(gcol)
                k_beta = k_h * bcol
                prob.append(dict(
                    c=c, h=h, q_dec=q_h * eg,
                    decay=jnp.where(causal, jnp.exp(jnp.where(causal, gcol - grow, 0.0)), 0.0),
                    lhs=jnp.concatenate([q_h, k_beta], axis=0).astype(BF16), k16=k_h.astype(BF16),
                    rhs=jnp.concatenate([k_beta * eg, vn_ref[rows, cols] * bcol], axis=1).astype(BF16),
                    k_dec=k_h * jnp.exp(glast - gcol), gl=jnp.exp(glast)))
        kks = [_dot_nt(p["lhs"], p["k16"]) for p in prob]
        a_list = [jnp.where(strict, kk[c_len:] * p["decay"], 0.0) for kk, p in zip(kks, prob)]
        t_invs = _unit_lower_inverses(a_list, eye, level_masks)
        wus = [_dot(t.astype(BF16), p["rhs"]).astype(BF16) for t, p in zip(t_invs, prob)]
        kd_wu = [_dot(p["k_dec"].T.astype(BF16), wu) for p, wu in zip(prob, wus)]
        qk_wu = [_dot((kk[:c_len] * p["decay"]).astype(BF16), wu) for p, kk, wu in zip(prob, kks, wus)]
        for p, kd, qk in zip(prob, kd_wu, qk_wu):
            c, h = p["c"], p["h"]
            lhs_ref[c, h] = jnp.concatenate(
                [kd[:, :B_HEAD_DIM], qk[:, :B_HEAD_DIM] - p["q_dec"]], axis=0).astype(BF16)
            ds_ref[c, h] = kd[:, B_HEAD_DIM:]
            o0_ref[c, h] = qk[:, B_HEAD_DIM:]
            gl_ref[c, h] = jnp.broadcast_to(p["gl"], (1, LANES))
        return carry

    lax.fori_loop(0, tb // (c_len * group), prepare, 0)

    def recur(c, carry):
        rows = pl.ds(pl.multiple_of(c * c_len, c_len), c_len)
        states = [state_ref[h] for h in heads]
        prods = [_dot(lhs_ref[c, h], states[h].astype(BF16)) for h in heads]
        for h in heads:
            state_ref[h] = states[h] * gl_ref[c, h] + (ds_ref[c, h] - prods[h][:B_HEAD_DIM])
            raw_ref[rows, h * LANES:(h + 1) * LANES] = o0_ref[c, h] - prods[h][B_HEAD_DIM:]
        return carry

    lax.fori_loop(0, tb // c_len, recur, 0)

    for h in heads:
        cols = slice(h * LANES, (h + 1) * LANES)
        z = z_ref[:, cols]
        o_ref[:, cols] = (_rms(raw_ref[:, cols], out_gain) * (z * jax.nn.sigmoid(z))).astype(BF16)


def _gdn(proj, alog_row, dtb_row, out_gain, *, bsz, seq, tb=512, group=8):
    t = proj.shape[0]
    nt = seq // tb

    def col(j):
        return pl.BlockSpec((tb, B_WIDTH), lambda b, i: (b * nt + i, j))

    nc = tb // B_CHUNK
    per_problem = lambda *shape_dtype: pltpu.VMEM((nc, B_HEADS) + shape_dtype[:-1], shape_dtype[-1])
    return pl.pallas_call(
        functools.partial(_gdn_kernel, tb=tb, group=group),
        out_shape=jax.ShapeDtypeStruct((t, B_WIDTH), BF16),
        grid=(bsz, nt),
        in_specs=[col(0), col(1), col(2), col(3),
                  pl.BlockSpec((tb, LANES), lambda b, i: (b * nt + i, 4 * B_WIDTH // LANES)),
                  pl.BlockSpec((1, LANES), lambda b, i: (0, 0)),
                  pl.BlockSpec((1, LANES), lambda b, i: (0, 0)),
                  pl.BlockSpec((1, B_HEAD_DIM), lambda b, i: (0, 0))],
        out_specs=pl.BlockSpec((tb, B_WIDTH), lambda b, i: (b * nt + i, 0)),
        scratch_shapes=[pltpu.VMEM((B_HEADS, B_HEAD_DIM, B_HEAD_DIM), F32),
                        per_problem(B_HEAD_DIM + B_CHUNK, B_HEAD_DIM, BF16),
                        per_problem(B_HEAD_DIM, B_HEAD_DIM, F32),
                        per_problem(B_CHUNK, B_HEAD_DIM, F32),
                        per_problem(1, LANES, F32),
                        pltpu.VMEM((tb, B_WIDTH), F32)],
        compiler_params=_params("parallel", "arbitrary"),
        name="gdn",
    )(proj, proj, proj, proj, proj, alog_row, dtb_row, out_gain)


def _in_odd_kernel(h_ref, g_ref, w_ref, qg_ref, kg_ref, o_ref, vt_ref, *, tm, tn, tk):
    hn = _rms(h_ref[...], g_ref[...]).astype(BF16)
    lane = lax.broadcasted_iota(jnp.int32, (tm, LANES), 1)
    low = lane < C_HEAD_DIM
    qk_width = 2 * C_HEADS * 2 * C_HEAD_DIM
    for start in range(0, 3 * C_HEADS * C_VALUE_DIM, tn):
        y = _dot(hn, w_ref[:, start:start + tn])
        if start >= qk_width:
            extra = jnp.where(lax.broadcasted_iota(jnp.int32, (V_ROWS - C_VALUE_DIM, tk), 0) == 0, 1.0, 0.0)
            for hh in range(tn // C_VALUE_DIM):
                row0 = ((start - qk_width) // C_VALUE_DIM + hh) * V_ROWS
                for r0 in range(0, tm, MXU_TILE):
                    vt_ref[r0 // tk, row0:row0 + C_VALUE_DIM, r0 % tk:r0 % tk + MXU_TILE] = (
                        y[r0:r0 + MXU_TILE, hh * C_VALUE_DIM:(hh + 1) * C_VALUE_DIM].T.astype(BF16))
                for kb in range(tm // tk):
                    vt_ref[kb, row0 + C_VALUE_DIM:row0 + V_ROWS, :] = extra.astype(BF16)
            continue
        is_q = start < qk_width // 2
        gain = qg_ref[...] if is_q else kg_ref[...]
        for j in range(tn // LANES):
            yb = y[:, j * LANES:(j + 1) * LANES]
            y2 = yb * yb
            lo = jnp.sum(jnp.where(low, y2, 0.0), axis=-1, keepdims=True)
            hi = jnp.sum(jnp.where(low, 0.0, y2), axis=-1, keepdims=True)
            ms = jnp.where(low, lo, hi) * (1.0 / C_HEAD_DIM)
            yn = yb * lax.rsqrt(ms + NORM_EPS) * gain
            if is_q:
                yn = yn * (C_HEAD_DIM ** -0.5 * LOG2_E)
            o_ref[:, start + j * LANES:start + (j + 1) * LANES] = yn.astype(BF16)


def _in_odd(h, gain, w, qg, kg, *, bsz, seq, tk, tm=512, tn=512):
    t = h.shape[0]
    n = w.shape[1]
    nt = seq // tm
    qk_width = 2 * C_HEADS * 2 * C_HEAD_DIM
    return pl.pallas_call(
        functools.partial(_in_odd_kernel, tm=tm, tn=tn, tk=tk),
        out_shape=(jax.ShapeDtypeStruct((t, qk_width), BF16),
                   jax.ShapeDtypeStruct((bsz, seq // tk, C_HEADS * V_ROWS, tk), BF16)),
        grid=(t // tm,),
        in_specs=[pl.BlockSpec((tm, D_MODEL), lambda i: (i, 0)),
                  _resident((1, D_MODEL)), _resident((D_MODEL, n)),
                  _resident((1, LANES)), _resident((1, LANES))],
        out_specs=(pl.BlockSpec((tm, qk_width), lambda i: (i, 0)),
                   pl.BlockSpec((None, tm // tk, C_HEADS * V_ROWS, tk), lambda i: (i // nt, i % nt, 0, 0))),
        compiler_params=_params("parallel"),
        name="in_odd",
    )(h, gain, w, qg, kg)


def _attn_kernel(slopes_ref, q_ref, k_ref, vt_ref, lam_ref, sg_ref, o_ref, s0_ref, s1_ref, mx0_ref,
                 mx1_ref, p_ref, q2_ref, rel_ref, m_ref, acc_ref, *, tq, tk, lam_init):
    slope = slopes_ref[pl.program_id(1)] * LOG2_E
    nq = q_ref.shape[0] // tq
    ki = lax.broadcasted_iota(jnp.int32, (tk, 2 * tq), 0)
    qj = lax.broadcasted_iota(jnp.int32, (tk, 2 * tq), 1)
    qj = jnp.where(qj >= tq, qj - tq, qj)
    rel_ref[...] = (ki - qj).astype(F32) * slope
    lane = lax.broadcasted_iota(jnp.int32, (tq, LANES), 1)
    lp = lam_ref[...]
    lam = (jnp.exp(jnp.sum(lp[0:1] * lp[1:2], axis=-1, keepdims=True))
           - jnp.exp(jnp.sum(lp[2:3] * lp[3:4], axis=-1, keepdims=True)) + lam_init)
    sub_gain = sg_ref[...]

    def load_q2(qi):
        q = q_ref[pl.ds(pl.multiple_of(qi * tq, tq), tq), :]
        zero = jnp.zeros_like(q)
        return jnp.concatenate([jnp.where(lane < C_HEAD_DIM, q, zero),
                                jnp.where(lane < C_HEAD_DIM, zero, q)], axis=0)

    lanes2 = 2 * tq
    n_tile = lanes2 // MXU_TILE
    n_slab = tk // MXU_TILE
    n_diag = tq // tk
    s_bufs = (s0_ref, s1_ref)
    mx_bufs = (mx0_ref, mx1_ref)
    chunk = ATTN_ROW_CHUNK

    def fold_max(x):
        return jnp.max(x.reshape(x.shape[0] // SUBLANES, SUBLANES, x.shape[1]), axis=0)

    def exp_rows(shift, par, rows, cols):
        p_ref[rows, cols] = jnp.exp2((s_bufs[par][rows, cols] + shift[:, cols]).astype(BF16))

    def produce_scores(par, k_blk, q2, j):
        cols = slice(j * MXU_TILE, (j + 1) * MXU_TILE)
        t = _dot_nt(k_blk, q2[cols, :]) + rel_ref[:, cols]
        s_bufs[par][:, cols] = t
        mx_bufs[par][:, cols] = fold_max(t)

    q2_first = load_q2(0)
    q2_ref[...] = q2_first
    for j in range(n_tile):
        produce_scores(0, k_ref[0:tk, :], q2_first, j)

    def q_block(qi, base):
        q0 = pl.multiple_of(qi * tq, tq)
        q2 = q2_ref[...]
        m_ref[...] = jnp.full(m_ref.shape, -jnp.inf, F32)
        acc_ref[...] = jnp.zeros(acc_ref.shape, F32)

        def step(kb, diag, par):
            if diag == n_diag - 1:
                q2_next = load_q2(jnp.minimum(qi + 1, nq - 1))
                q2_ref[...] = q2_next
                k_next = k_ref[0:tk, :]
            else:
                q2_next = q2
                k_next = k_ref[pl.ds(pl.multiple_of((kb + 1) * tk, tk), tk), :]

            def scores_piece(j):
                produce_scores(1 - par, k_next, q2_next, j)

            def lead(j):
                return (j * MXU_TILE) % tq - diag * tk

            def live_slabs(j):
                if diag is None:
                    return n_slab
                return min(max(lead(j) // MXU_TILE + 1, 0), n_slab)

            scores_piece(0)
            if diag is None:
                mx = mx_bufs[par][...]
            else:
                mx_tiles = []
                for j in range(n_tile):
                    cols = slice(j * MXU_TILE, (j + 1) * MXU_TILE)
                    mx_j = jnp.full((SUBLANES, MXU_TILE), -jnp.inf, F32)
                    for r0 in range(0, live_slabs(j) * MXU_TILE, chunk):
                        t = s_bufs[par][r0:r0 + chunk, cols]
                        if r0 // MXU_TILE == lead(j) // MXU_TILE:
                            krow = lax.broadcasted_iota(jnp.int32, (chunk, MXU_TILE), 0) + r0
                            qcol = lax.broadcasted_iota(jnp.int32, (chunk, MXU_TILE), 1) + lead(j)
                            t = jnp.where(krow <= qcol, t, -jnp.inf)
                            s_bufs[par][r0:r0 + chunk, cols] = t
                        mx_j = jnp.maximum(mx_j, fold_max(t))
                    mx_tiles.append(mx_j)
                mx = jnp.concatenate(mx_tiles, axis=1)
            offset = slope * (kb * tk - q0).astype(F32)
            m_prev = m_ref[...]
            m_new = jnp.maximum(m_prev, jnp.max(mx, axis=0, keepdims=True) + offset)
            alpha = jnp.exp2(m_prev - m_new)
            shift = offset - m_new
            pv = [None] * n_tile
            for slab in range(n_slab):
                keys = slice(slab * MXU_TILE, (slab + 1) * MXU_TILE)
                tiles = [j for j in range(n_tile) if slab < live_slabs(j)]
                for r0 in range(keys.start, keys.stop, chunk):
                    for j in tiles:
                        exp_rows(shift, par, slice(r0, r0 + chunk), slice(j * MXU_TILE, (j + 1) * MXU_TILE))
                first = 1 + slab * (n_tile - 1) // n_slab
                for j in range(first, 1 + (slab + 1) * (n_tile - 1) // n_slab):
                    scores_piece(j)
                for j in tiles:
                    part = _dot(vt_ref[kb][:, keys], p_ref[keys, j * MXU_TILE:(j + 1) * MXU_TILE])
                    pv[j] = part if pv[j] is None else pv[j] + part
            dead = jnp.zeros((V_ROWS, MXU_TILE), F32)
            pv = [dead if part is None else part for part in pv]
            acc_ref[...] = alpha * acc_ref[...] + jnp.concatenate(pv, axis=1)
            m_ref[...] = m_new

        def either_parity(kb, diag):
            dyn_par = (base + kb) % 2
            for par in range(2):
                pl.when(dyn_par == par)(functools.partial(step, kb, diag, par))

        def body(kb, c):
            either_parity(kb, None)
            return c

        lax.fori_loop(0, qi * n_diag, body, 0)
        for diag in range(n_diag):
            either_parity(qi * n_diag + diag, diag)
        acc = acc_ref[...]
        o2 = acc[:C_VALUE_DIM] / acc[C_VALUE_DIM:C_VALUE_DIM + 1]
        o_t = o2[:, :tq] - lam * o2[:, tq:]
        o_t = o_t * lax.rsqrt(jnp.mean(o_t * o_t, axis=0, keepdims=True) + NORM_EPS)
        o_ref[pl.ds(q0, tq), :] = (o_t.T * sub_gain * (1.0 - lam_init)).astype(BF16)
        return (base + (qi + 1) * n_diag) % 2

    lax.fori_loop(0, nq, q_block, jnp.int32(0))


def _attn(qk, vt, slopes, lam_params, sub_gain, *, bsz, seq, lam_init, tq):
    t = qk.shape[0]
    tk = vt.shape[3]
    assert tq % tk == 0
    kernel = functools.partial(_attn_kernel, tq=tq, tk=tk, lam_init=lam_init)
    return pl.pallas_call(
        kernel,
        out_shape=jax.ShapeDtypeStruct((t, C_HEADS * C_VALUE_DIM), BF16),
        grid_spec=pltpu.PrefetchScalarGridSpec(
            num_scalar_prefetch=1,
            grid=(bsz, C_HEADS),
            in_specs=[pl.BlockSpec((seq, LANES), lambda b, h, s: (b, h)),
                      pl.BlockSpec((seq, LANES), lambda b, h, s: (b, C_HEADS + h)),
                      pl.BlockSpec((None, seq // tk, V_ROWS, tk), lambda b, h, s: (b, 0, h, 0)),
                      pl.BlockSpec((4, C_HEAD_DIM), lambda b, h, s: (0, 0)),
                      pl.BlockSpec((1, C_VALUE_DIM), lambda b, h, s: (0, 0))],
            out_specs=pl.BlockSpec((seq, LANES), lambda b, h, s: (b, h)),
            scratch_shapes=[pltpu.VMEM((tk, 2 * tq), F32), pltpu.VMEM((tk, 2 * tq), F32),
                            pltpu.VMEM((SUBLANES, 2 * tq), F32), pltpu.VMEM((SUBLANES, 2 * tq), F32),
                            pltpu.VMEM((tk, 2 * tq), BF16),
                            pltpu.VMEM((2 * tq, LANES), BF16),
                            pltpu.VMEM((tk, 2 * tq), F32),
                            pltpu.VMEM((1, 2 * tq), F32),
                            pltpu.VMEM((V_ROWS, 2 * tq), F32)]),
        compiler_params=_params("parallel", "parallel"),
        name="diff_attn",
    )(slopes, qk, qk, vt, lam_params, sub_gain)


def _post_kernel(*refs, n_mix, n_convert, fc):
    h_ref = refs[0]
    mix_refs = refs[1:1 + n_mix]
    (w_out_ref, g_mlp_ref, w1_ref, w2_ref, g_ple_ref, w_gate_ref, w_proj_ref, p_ref) = refs[1 + n_mix:9 + n_mix]
    f32_refs = refs[9 + n_mix:9 + n_mix + n_convert]
    o_ref = refs[9 + n_mix + n_convert]
    for src_ref, dst_ref in zip(f32_refs, refs[10 + n_mix + n_convert:]):
        dst_ref[...] = src_ref[...].astype(BF16)
    mix = mix_refs[0][...] if n_mix == 1 else jnp.concatenate([m[...] for m in mix_refs], axis=1)
    h = h_ref[...] + _dot(mix, w_out_ref[...])
    hn = _rms(h, g_mlp_ref[...]).astype(BF16)
    acc = h
    for c in range(0, D_FF, fc):
        a = jnp.maximum(_dot(hn, w1_ref[:, c:c + fc]), 0.0)
        acc = acc + _dot((a * a).astype(BF16), w2_ref[c:c + fc, :])
    h = acc
    gate = jax.nn.sigmoid(_dot(_rms(h, g_ple_ref[...]).astype(BF16), w_gate_ref[...]))
    o_ref[...] = h + _dot(p_ref[...].astype(BF16), w_proj_ref[...]) * gate


def _post(h, mixes, w_out, g_mlp, w1, w2, g_ple, w_gate, w_proj_all, p_all, layer, convert=(), *,
          tm=512, fc=512):
    t = h.shape[0]
    steps = t // tm
    row = lambda i: (i, 0)
    mix_specs = [pl.BlockSpec((tm, m.shape[1]), row) for m in mixes]
    in_specs = [pl.BlockSpec((tm, D_MODEL), row)] + mix_specs + [
        _resident((D_MODEL, D_MODEL)), _resident((1, D_MODEL)),
        _resident((D_MODEL, D_FF)), _resident((D_FF, D_MODEL)),
        _resident((1, D_MODEL)), _resident((D_MODEL, D_MODEL)),
        _stacked((PLE_DIM, D_MODEL), layer),
        pl.BlockSpec((None, tm, PLE_DIM), lambda i: (layer, i, 0))]
    out_shape = [jax.ShapeDtypeStruct((t, D_MODEL), F32)]
    out_specs = [pl.BlockSpec((tm, D_MODEL), row)]
    args = [h, *mixes, w_out, g_mlp, w1, w2, g_ple, w_gate, w_proj_all, p_all]
    for w_all, index in convert:
        rows, cols = w_all.shape[1] // steps, w_all.shape[2]
        in_specs.append(pl.BlockSpec((None, rows, cols), lambda i, index=index: (index, i, 0)))
        out_shape.append(jax.ShapeDtypeStruct(w_all.shape[1:], BF16))
        out_specs.append(pl.BlockSpec((rows, cols), row))
        args.append(w_all)
    return pl.pallas_call(
        functools.partial(_post_kernel, n_mix=len(mixes), n_convert=len(convert), fc=fc),
        out_shape=out_shape,
        grid=(steps,),
        in_specs=in_specs,
        out_specs=out_specs,
        compiler_params=_params("parallel"),
        name="post",
    )(*args)


def _lane_row(values, offset):
    return jnp.zeros((1, LANES), F32).at[0, offset:offset + values.shape[0]].set(values.astype(F32))


def kernel(x, p, ln_mix_e, w_in_e, gmlp_v_gain, gmlp_ws, gmlp_bs, gdn_conv, gdn_a_log, gdn_dt_bias,
           gdn_out_gain, w_out_e, ln_mix_o, w_qkv_o, attn_q_gain, attn_k_gain, diff_lambda,
           attn_sub_gain, w_out_o, ln_mlp, w_mlp1, w_mlp2, ln_ple, w_ple_gate, w_ple_proj):
    bsz, seq, d = x.shape
    depth = p.shape[0]
    t = bsz * seq
    h = x.reshape(t, d)
    p_all = p.reshape(depth, t, PLE_DIM)
    slopes = 2.0 ** (-8.0 * jnp.arange(1, C_HEADS + 1, dtype=F32) / C_HEADS)
    row = lambda a: a.reshape(1, -1).astype(F32)
    w_in_e16 = jnp.pad(w_in_e, ((0, 0), (0, 0), (0, EVEN_PAD - w_in_e.shape[2]))).astype(BF16)
    w_proj16 = w_ple_proj.astype(BF16)
    w1, w2, w_gate, w_out = (w[0].astype(BF16) for w in (w_mlp1, w_mlp2, w_ple_gate, w_out_e))
    w_qkv = None
    for layer in range(depth):
        i = layer // 2
        if layer % 2 == 0:
            a_out, proj = _in_even(h, row(ln_mix_e[i]), w_in_e16, row(gmlp_v_gain[i]), gmlp_ws,
                                   gmlp_bs[i].T, gdn_conv, i, seq=seq)
            b_out = _gdn(proj, _lane_row(gdn_a_log[i], B_HEADS), _lane_row(gdn_dt_bias[i], B_HEADS),
                         row(gdn_out_gain[i]), bsz=bsz, seq=seq)
            mixes = (a_out, b_out)
        else:
            lam_init = 0.8 - 0.6 * math.exp(-0.3 * layer)
            qg = row(jnp.concatenate([attn_q_gain[i], attn_q_gain[i]]))
            kg = row(jnp.concatenate([attn_k_gain[i], attn_k_gain[i]]))
            qk, vt = _in_odd(h, row(ln_mix_o[i]), w_qkv, qg, kg, bsz=bsz, seq=seq, tk=ATTN_K_BLOCK)
            o = _attn(qk, vt, slopes, diff_lambda[i], row(attn_sub_gain[i]), bsz=bsz, seq=seq,
                      lam_init=lam_init, tq=ATTN_Q_BLOCK)
            mixes = (o,)
        nxt = layer + 1
        convert = []
        if nxt < depth:
            convert = [(w_mlp1, nxt), (w_mlp2, nxt), (w_ple_gate, nxt),
                       (w_out_o if nxt % 2 else w_out_e, nxt // 2)]
            if nxt % 2:
                convert.append((w_qkv_o, nxt // 2))
        h, *converted = _post(h, mixes, w_out, row(ln_mlp[layer]), w1, w2, row(ln_ple[layer]), w_gate,
                              w_proj16, p_all, layer, convert)
        if converted:
            w1, w2, w_gate, w_out = converted[:4]
            w_qkv = converted[4] if nxt % 2 else None
    return h.reshape(bsz, seq, d)
```

```python
import functools
import math

import jax
import jax.numpy as jnp
from jax import lax
from jax.experimental import pallas as pl
from jax.experimental.pallas import tpu as pltpu

F32 = jnp.float32
BF16 = jnp.bfloat16
NORM_EPS = 1e-6

LANES = 128
SUBLANES = 8
MXU_TILE = 256
D_MODEL = 1024
PLE_DIM = 256
D_FF = 4 * D_MODEL
A_WIDTH = 512
A_GROUPS = 4
A_CHUNK = 128
B_HEADS = 4
B_HEAD_DIM = 128
B_WIDTH = 512
B_CONV = 4
B_CHUNK = 64
CONV_PAD = 8
C_HEADS = 8
C_HEAD_DIM = 64
C_VALUE_DIM = 128
ATTN_Q_BLOCK = 512
ATTN_K_BLOCK = 512
ATTN_ROW_CHUNK = 32
V_ROWS = C_VALUE_DIM + 16
LOG2_E = math.log2(math.e)
EVEN_MAIN = 2 * A_WIDTH + 4 * B_WIDTH
EVEN_PAD = EVEN_MAIN + LANES
VMEM_LIMIT = 56 * 1024 * 1024


def _rms(x, gain):
    return x * lax.rsqrt(jnp.mean(x * x, axis=-1, keepdims=True) + NORM_EPS) * gain


def _dot(a, b):
    return jnp.dot(a, b, preferred_element_type=F32)


def _dot_nt(a, b):
    return lax.dot_general(a, b, (((1,), (1,)), ((), ())), preferred_element_type=F32)


def _resident(shape):
    zeros = (0,) * len(shape)
    return pl.BlockSpec(shape, lambda *_: zeros, pipeline_mode=pl.Buffered(1))


def _stacked(shape, index):
    zeros = (0,) * len(shape)
    return pl.BlockSpec((None,) + shape, lambda *_: (index,) + zeros, pipeline_mode=pl.Buffered(1))


def _params(*sem):
    return pltpu.CompilerParams(dimension_semantics=sem, vmem_limit_bytes=VMEM_LIMIT)


def _in_even_kernel(h_ref, g_ref, w_ref, vg_ref, ws_ref, bs_ref, cw_ref, a_ref, o_ref, buf_ref, *,
                    tm, tiles_per_seq):
    @pl.when(pl.program_id(0) % tiles_per_seq == 0)
    def _():
        buf_ref[:, 0:CONV_PAD, :] = jnp.zeros((3, CONV_PAD, B_WIDTH), F32)

    hn = _rms(h_ref[...], g_ref[...]).astype(BF16)
    u = jax.nn.gelu(_dot(hn, w_ref[:, 0:A_WIDTH]))
    v = jax.nn.gelu(_dot(hn, w_ref[:, A_WIDTH:2 * A_WIDTH]))
    for part in range(3):
        cols = slice(part * B_WIDTH, (part + 1) * B_WIDTH)
        x = _dot(hn, w_ref[:, 2 * A_WIDTH + cols.start:2 * A_WIDTH + cols.stop])
        buf_ref[part, CONV_PAD:CONV_PAD + tm, :] = x
        cw = cw_ref[:, cols]
        y = cw[B_CONV - 1:B_CONV, :] * x
        for j in range(B_CONV - 1):
            off = CONV_PAD - (B_CONV - 1) + j
            y = y + cw[j:j + 1, :] * buf_ref[part, off:off + tm, :]
        buf_ref[part, 0:CONV_PAD, :] = x[tm - CONV_PAD:tm, :]
        half = 0.5 * y
        y = half + half * jnp.tanh(half)
        if part == 2:
            o_ref[:, cols] = y
        else:
            for h in range(B_HEADS):
                hcols = slice(cols.start + h * LANES, cols.start + (h + 1) * LANES)
                yh = y[:, h * LANES:(h + 1) * LANES]
                yh = yh * lax.rsqrt(jnp.sum(yh * yh, axis=-1, keepdims=True) + NORM_EPS)
                o_ref[:, hcols] = yh * (B_HEAD_DIM ** -0.5) if part == 0 else yh
    rest = 2 * A_WIDTH + 3 * B_WIDTH
    o_ref[:, 3 * B_WIDTH:] = _dot(hn, w_ref[:, rest:])
    ii = lax.broadcasted_iota(jnp.int32, (A_CHUNK, A_CHUNK), 0)
    jj = lax.broadcasted_iota(jnp.int32, (A_CHUNK, A_CHUNK), 1)
    for g in range(A_GROUPS):
        cols = slice(g * LANES, (g + 1) * LANES)
        w = jnp.where(ii >= jj, ws_ref[g], 0.0).astype(BF16)
        bias = bs_ref[:, g:g + 1]
        gain = vg_ref[:, cols]
        for c in range(tm // A_CHUNK):
            rows = slice(c * A_CHUNK, (c + 1) * A_CHUNK)
            vg = v[rows, cols]
            vc = vg - jnp.mean(vg, axis=-1, keepdims=True)
            y = vc * lax.rsqrt(jnp.mean(vc * vc, axis=-1, keepdims=True) + NORM_EPS) * gain
            s = _dot(w, y.astype(BF16)) + bias
            a_ref[rows, cols] = (u[rows, cols] * s).astype(BF16)


def _in_even(h, gain, w_all, v_gain, ws_all, bs_t, conv_all, i, *, seq, tm=512):
    t = h.shape[0]
    return pl.pallas_call(
        functools.partial(_in_even_kernel, tm=tm, tiles_per_seq=seq // tm),
        out_shape=(jax.ShapeDtypeStruct((t, A_WIDTH), BF16),
                   jax.ShapeDtypeStruct((t, EVEN_PAD - 2 * A_WIDTH), F32)),
        grid=(t // tm,),
        in_specs=[pl.BlockSpec((tm, D_MODEL), lambda i_: (i_, 0)),
                  _resident((1, D_MODEL)), _stacked((D_MODEL, EVEN_PAD), i),
                  _resident((1, A_WIDTH)), _stacked((A_GROUPS, A_CHUNK, A_CHUNK), i),
                  _resident((A_CHUNK, A_GROUPS)), _stacked((B_CONV, 3 * B_WIDTH), i)],
        out_specs=(pl.BlockSpec((tm, A_WIDTH), lambda i_: (i_, 0)),
                   pl.BlockSpec((tm, EVEN_PAD - 2 * A_WIDTH), lambda i_: (i_, 0))),
        scratch_shapes=[pltpu.VMEM((3, CONV_PAD + tm, B_WIDTH), F32)],
        compiler_params=_params("arbitrary"),
        name="in_even",
    )(h, gain, w_all, v_gain, ws_all, bs_t, conv_all)


def _unit_lower_inverses(a_list, eye, level_masks):
    ds = [eye - jnp.where(level_masks[0], a, 0.0) for a in a_list]
    for mask in level_masks[1:]:
        d16 = [d.astype(BF16) for d in ds]
        mds = [_dot(jnp.where(mask, a, 0.0).astype(BF16), d) for a, d in zip(a_list, d16)]
        ds = [d - _dot(d_lo, md.astype(BF16)) for d, d_lo, md in zip(ds, d16, mds)]
    return ds


def _gdn_kernel(qn_ref, kn_ref, vn_ref, z_ref, ba_ref, alog_ref, dtb_ref, og_ref, o_ref, state_ref,
                lhs_ref, ds_ref, o0_ref, gl_ref, raw_ref, *, tb, group):
    c_len = B_CHUNK

    @pl.when(pl.program_id(1) == 0)
    def _():
        state_ref[...] = jnp.zeros_like(state_ref)

    ii = lax.broadcasted_iota(jnp.int32, (c_len, c_len), 0)
    jj = lax.broadcasted_iota(jnp.int32, (c_len, c_len), 1)
    causal = ii >= jj
    strict = ii > jj
    eye = jnp.where(ii == jj, 1.0, 0.0).astype(F32)
    ltri = jnp.where(causal, 1.0, 0.0).astype(BF16)
    level_masks = []
    s = 1
    while s < c_len:
        level_masks.append(((ii // (2 * s)) == (jj // (2 * s))) & ((ii % (2 * s)) >= s) & ((jj % (2 * s)) < s))
        s *= 2
    neg_rate = -jnp.exp(alog_ref[...])
    dt_bias = dtb_ref[...]
    out_gain = og_ref[...]

    heads = range(B_HEADS)

    def prepare(gi, carry):
        prob = []
        for cc in range(group):
            c = gi * group + cc
            rows = pl.ds(pl.multiple_of(c * c_len, c_len), c_len)
            ba = ba_ref[rows, :]
            beta = jax.nn.sigmoid(ba)
            g = neg_rate * jax.nn.softplus(ba + dt_bias)
            g_hi = g.astype(BF16)
            g_lo = (g - g_hi.astype(F32)).astype(BF16)
            gc = _dot(ltri, g_hi) + _dot(ltri, g_lo)
            gc_t = gc.T
            for h in heads:
                cols = slice(h * LANES, (h + 1) * LANES)
                q_h = qn_ref[rows, cols]
                k_h = kn_ref[rows, cols]
                gcol = gc[:, B_HEADS + h:B_HEADS + h + 1]
                grow = gc_t[B_HEADS + h:B_HEADS + h + 1, :]
                glast = gc[c_len - 1:c_len, B_HEADS + h:B_HEADS + h + 1]
                bcol = beta[:, h:h + 1]
                decay = jnp.where(causal, jnp.exp(jnp.where(causal, gcol - grow, 0.0)), 0.0)
                kk = _dot_nt(jnp.concatenate([q_h, k_h * bcol], axis=0).astype(BF16), k_h.astype(BF16))
                prob.append(dict(c=c, h=h, rows=rows, cols=cols, gc=gc, beta=beta,
                                 a=jnp.where(strict, kk[c_len:] * decay, 0.0),
                                 qk=(kk[:c_len] * decay).astype(BF16)))
        t_invs = _unit_lower_inverses([p["a"] for p in prob], eye, level_masks)
        def columns(p):
            lane = B_HEADS + p["h"]
            return (p["gc"][:, lane:lane + 1], p["gc"][c_len - 1:c_len, lane:lane + 1],
                    p["beta"][:, p["h"]:p["h"] + 1])

        wus, k_decs = [], []
        for t_inv, p in zip(t_invs, prob):
            gcol, glast, bcol = columns(p)
            k_h = kn_ref[p["rows"], p["cols"]]
            rhs = jnp.concatenate([k_h * bcol * jnp.exp(gcol), vn_ref[p["rows"], p["cols"]] * bcol], axis=1)
            wus.append(_dot(t_inv.astype(BF16), rhs.astype(BF16)).astype(BF16))
            k_decs.append(k_h * jnp.exp(glast - gcol))
        kd_wu = [_dot(k_dec.T.astype(BF16), wu) for k_dec, wu in zip(k_decs, wus)]
        qk_wu = [_dot(p["qk"], wu) for p, wu in zip(prob, wus)]
        for p, kd, qk in zip(prob, kd_wu, qk_wu):
            c, h = p["c"], p["h"]
            gcol, glast, _ = columns(p)
            q_dec = qn_ref[p["rows"], p["cols"]] * jnp.exp(gcol)
            lhs_ref[c, h] = jnp.concatenate(
                [kd[:, :B_HEAD_DIM], qk[:, :B_HEAD_DIM] - q_dec], axis=0).astype(BF16)
            ds_ref[c, h] = kd[:, B_HEAD_DIM:]
            o0_ref[c, h] = qk[:, B_HEAD_DIM:]
            gl_ref[c, h] = jnp.broadcast_to(jnp.exp(glast), (1, LANES))
        return carry

    lax.fori_loop(0, tb // (c_len * group), prepare, 0)

    def recur(c, carry):
        rows = pl.ds(pl.multiple_of(c * c_len, c_len), c_len)
        states = [state_ref[h] for h in heads]
        prods = [_dot(lhs_ref[c, h], states[h].astype(BF16)) for h in heads]
        for h in heads:
            state_ref[h] = states[h] * gl_ref[c, h] + (ds_ref[c, h] - prods[h][:B_HEAD_DIM])
            raw_ref[rows, h * LANES:(h + 1) * LANES] = o0_ref[c, h] - prods[h][B_HEAD_DIM:]
        return carry

    lax.fori_loop(0, tb // c_len, recur, 0)

    for h in heads:
        cols = slice(h * LANES, (h + 1) * LANES)
        z = z_ref[:, cols]
        o_ref[:, cols] = (_rms(raw_ref[:, cols], out_gain) * (z * jax.nn.sigmoid(z))).astype(BF16)


def _gdn(proj, alog_row, dtb_row, out_gain, *, bsz, seq, tb=512, group=8):
    t = proj.shape[0]
    nt = seq // tb

    def col(j):
        return pl.BlockSpec((tb, B_WIDTH), lambda b, i: (b * nt + i, j))

    nc = tb // B_CHUNK
    per_problem = lambda *shape_dtype: pltpu.VMEM((nc, B_HEADS) + shape_dtype[:-1], shape_dtype[-1])
    return pl.pallas_call(
        functools.partial(_gdn_kernel, tb=tb, group=group),
        out_shape=jax.ShapeDtypeStruct((t, B_WIDTH), BF16),
        grid=(bsz, nt),
        in_specs=[col(0), col(1), col(2), col(3),
                  pl.BlockSpec((tb, LANES), lambda b, i: (b * nt + i, 4 * B_WIDTH // LANES)),
                  pl.BlockSpec((1, LANES), lambda b, i: (0, 0)),
                  pl.BlockSpec((1, LANES), lambda b, i: (0, 0)),
                  pl.BlockSpec((1, B_HEAD_DIM), lambda b, i: (0, 0))],
        out_specs=pl.BlockSpec((tb, B_WIDTH), lambda b, i: (b * nt + i, 0)),
        scratch_shapes=[pltpu.VMEM((B_HEADS, B_HEAD_DIM, B_HEAD_DIM), F32),
                        per_problem(B_HEAD_DIM + B_CHUNK, B_HEAD_DIM, BF16),
                        per_problem(B_HEAD_DIM, B_HEAD_DIM, F32),
                        per_problem(B_CHUNK, B_HEAD_DIM, F32),
                        per_problem(1, LANES, F32),
                        pltpu.VMEM((tb, B_WIDTH), F32)],
        compiler_params=_params("parallel", "arbitrary"),
        name="gdn",
    )(proj, proj, proj, proj, proj, alog_row, dtb_row, out_gain)


def _in_odd_kernel(h_ref, g_ref, w_ref, qg_ref, kg_ref, o_ref, vt_ref, *, tm, tn, tk):
    hn = _rms(h_ref[...], g_ref[...]).astype(BF16)
    lane = lax.broadcasted_iota(jnp.int32, (tm, LANES), 1)
    low = lane < C_HEAD_DIM
    qk_width = 2 * C_HEADS * 2 * C_HEAD_DIM
    for start in range(0, 3 * C_HEADS * C_VALUE_DIM, tn):
        y = _dot(hn, w_ref[:, start:start + tn])
        if start >= qk_width:
            extra = jnp.where(lax.broadcasted_iota(jnp.int32, (V_ROWS - C_VALUE_DIM, tk), 0) == 0, 1.0, 0.0)
            for hh in range(tn // C_VALUE_DIM):
                row0 = ((start - qk_width) // C_VALUE_DIM + hh) * V_ROWS
                for r0 in range(0, tm, MXU_TILE):
                    vt_ref[r0 // tk, row0:row0 + C_VALUE_DIM, r0 % tk:r0 % tk + MXU_TILE] = (
                        y[r0:r0 + MXU_TILE, hh * C_VALUE_DIM:(hh + 1) * C_VALUE_DIM].T.astype(BF16))
                for kb in range(tm // tk):
                    vt_ref[kb, row0 + C_VALUE_DIM:row0 + V_ROWS, :] = extra.astype(BF16)
            continue
        is_q = start < qk_width // 2
        gain = qg_ref[...] if is_q else kg_ref[...]
        for j in range(tn // LANES):
            yb = y[:, j * LANES:(j + 1) * LANES]
            y2 = yb * yb
            lo = jnp.sum(jnp.where(low, y2, 0.0), axis=-1, keepdims=True)
            hi = jnp.sum(jnp.where(low, 0.0, y2), axis=-1, keepdims=True)
            ms = jnp.where(low, lo, hi) * (1.0 / C_HEAD_DIM)
            yn = yb * lax.rsqrt(ms + NORM_EPS) * gain
            if is_q:
                yn = yn * (C_HEAD_DIM ** -0.5 * LOG2_E)
            o_ref[:, start + j * LANES:start + (j + 1) * LANES] = yn.astype(BF16)


def _in_odd(h, gain, w, qg, kg, *, bsz, seq, tk, tm=512, tn=512):
    t = h.shape[0]
    n = w.shape[1]
    nt = seq // tm
    qk_width = 2 * C_HEADS * 2 * C_HEAD_DIM
    return pl.pallas_call(
        functools.partial(_in_odd_kernel, tm=tm, tn=tn, tk=tk),
        out_shape=(jax.ShapeDtypeStruct((t, qk_width), BF16),
                   jax.ShapeDtypeStruct((bsz, seq // tk, C_HEADS * V_ROWS, tk), BF16)),
        grid=(t // tm,),
        in_specs=[pl.BlockSpec((tm, D_MODEL), lambda i: (i, 0)),
                  _resident((1, D_MODEL)), _resident((D_MODEL, n)),
                  _resident((1, LANES)), _resident((1, LANES))],
        out_specs=(pl.BlockSpec((tm, qk_width), lambda i: (i, 0)),
                   pl.BlockSpec((None, tm // tk, C_HEADS * V_ROWS, tk), lambda i: (i // nt, i % nt, 0, 0))),
        compiler_params=_params("parallel"),
        name="in_odd",
    )(h, gain, w, qg, kg)


def _attn_kernel(slopes_ref, q_ref, k_ref, vt_ref, lam_ref, sg_ref, o_ref, s0_ref, s1_ref, mx0_ref,
                 mx1_ref, p_ref, q2_ref, rel_ref, m_ref, acc_ref, *, tq, tk, lam_init):
    slope = slopes_ref[pl.program_id(1)] * LOG2_E
    nq = q_ref.shape[0] // tq
    ki = lax.broadcasted_iota(jnp.int32, (tk, 2 * tq), 0)
    qj = lax.broadcasted_iota(jnp.int32, (tk, 2 * tq), 1)
    qj = jnp.where(qj >= tq, qj - tq, qj)
    rel_ref[...] = (ki - qj).astype(F32) * slope
    lane = lax.broadcasted_iota(jnp.int32, (tq, LANES), 1)
    lp = lam_ref[...]
    lam = (jnp.exp(jnp.sum(lp[0:1] * lp[1:2], axis=-1, keepdims=True))
           - jnp.exp(jnp.sum(lp[2:3] * lp[3:4], axis=-1, keepdims=True)) + lam_init)
    sub_gain = sg_ref[...]

    def load_q2(qi):
        q = q_ref[pl.ds(pl.multiple_of(qi * tq, tq), tq), :]
        zero = jnp.zeros_like(q)
        return jnp.concatenate([jnp.where(lane < C_HEAD_DIM, q, zero),
                                jnp.where(lane < C_HEAD_DIM, zero, q)], axis=0)

    lanes2 = 2 * tq
    n_tile = lanes2 // MXU_TILE
    n_slab = tk // MXU_TILE
    n_diag = tq // tk
    s_bufs = (s0_ref, s1_ref)
    mx_bufs = (mx0_ref, mx1_ref)
    chunk = ATTN_ROW_CHUNK

    def fold_max(x):
        return jnp.max(x.reshape(x.shape[0] // SUBLANES, SUBLANES, x.shape[1]), axis=0)

    def exp_rows(shift, par, rows, cols):
        p_ref[rows, cols] = jnp.exp2((s_bufs[par][rows, cols] + shift[:, cols]).astype(BF16))

    def produce_scores(par, k_blk, q2, j):
        cols = slice(j * MXU_TILE, (j + 1) * MXU_TILE)
        t = _dot_nt(k_blk, q2[cols, :]) + rel_ref[:, cols]
        s_bufs[par][:, cols] = t
        mx_bufs[par][:, cols] = fold_max(t)

    q2_first = load_q2(0)
    q2_ref[...] = q2_first
    for j in range(n_tile):
        produce_scores(0, k_ref[0:tk, :], q2_first, j)

    def q_block(qi, base):
        q0 = pl.multiple_of(qi * tq, tq)
        q2 = q2_ref[...]
        m_ref[...] = jnp.full(m_ref.shape, -jnp.inf, F32)
        acc_ref[...] = jnp.zeros(acc_ref.shape, F32)

        def step(kb, diag, par):
            if diag == n_diag - 1:
                q2_next = load_q2(jnp.minimum(qi + 1, nq - 1))
                q2_ref[...] = q2_next
                k_next = k_ref[0:tk, :]
            else:
                q2_next = q2
                k_next = k_ref[pl.ds(pl.multiple_of((kb + 1) * tk, tk), tk), :]

            def scores_piece(j):
                produce_scores(1 - par, k_next, q2_next, j)

            def lead(j):
                return (j * MXU_TILE) % tq - diag * tk

            def live_slabs(j):
                if diag is None:
                    return n_slab
                return min(max(lead(j) // MXU_TILE + 1, 0), n_slab)

            scores_piece(0)
            if diag is None:
                mx = mx_bufs[par][...]
            else:
                mx_tiles = []
                for j in range(n_tile):
                    cols = slice(j * MXU_TILE, (j + 1) * MXU_TILE)
                    mx_j = jnp.full((SUBLANES, MXU_TILE), -jnp.inf, F32)
                    for r0 in range(0, live_slabs(j) * MXU_TILE, chunk):
                        t = s_bufs[par][r0:r0 + chunk, cols]
                        if r0 // MXU_TILE == lead(j) // MXU_TILE:
                            krow = lax.broadcasted_iota(jnp.int32, (chunk, MXU_TILE), 0) + r0
                            qcol = lax.broadcasted_iota(jnp.int32, (chunk, MXU_TILE), 1) + lead(j)
                            t = jnp.where(krow <= qcol, t, -jnp.inf)
                            s_bufs[par][r0:r0 + chunk, cols] = t
                        mx_j = jnp.maximum(mx_j, fold_max(t))
                    mx_tiles.append(mx_j)
                mx = jnp.concatenate(mx_tiles, axis=1)
            offset = slope * (kb * tk - q0).astype(F32)
            m_prev = m_ref[...]
            m_new = jnp.maximum(m_prev, jnp.max(mx, axis=0, keepdims=True) + offset)
            alpha = jnp.exp2(m_prev - m_new)
            shift = offset - m_new
            pv = [None] * n_tile
            for slab in range(n_slab):
                keys = slice(slab * MXU_TILE, (slab + 1) * MXU_TILE)
                tiles = [j for j in range(n_tile) if slab < live_slabs(j)]
                for r0 in range(keys.start, keys.stop, chunk):
                    for j in tiles:
                        exp_rows(shift, par, slice(r0, r0 + chunk), slice(j * MXU_TILE, (j + 1) * MXU_TILE))
                first = 1 + slab * (n_tile - 1) // n_slab
                for j in range(first, 1 + (slab + 1) * (n_tile - 1) // n_slab):
                    scores_piece(j)
                for j in tiles:
                    part = _dot(vt_ref[kb][:, keys], p_ref[keys, j * MXU_TILE:(j + 1) * MXU_TILE])
                    pv[j] = part if pv[j] is None else pv[j] + part
            dead = jnp.zeros((V_ROWS, MXU_TILE), F32)
            pv = [dead if part is None else part for part in pv]
            acc_ref[...] = alpha * acc_ref[...] + jnp.concatenate(pv, axis=1)
            m_ref[...] = m_new

        def either_parity(kb, diag):
            dyn_par = (base + kb) % 2
            for par in range(2):
                pl.when(dyn_par == par)(functools.partial(step, kb, diag, par))

        def body(kb, c):
            either_parity(kb, None)
            return c

        lax.fori_loop(0, qi * n_diag, body, 0)
        for diag in range(n_diag):
            either_parity(qi * n_diag + diag, diag)
        acc = acc_ref[...]
        o2 = acc[:C_VALUE_DIM] / acc[C_VALUE_DIM:C_VALUE_DIM + 1]
        o_t = o2[:, :tq] - lam * o2[:, tq:]
        o_t = o_t * lax.rsqrt(jnp.mean(o_t * o_t, axis=0, keepdims=True) + NORM_EPS)
        o_ref[pl.ds(q0, tq), :] = (o_t.T * sub_gain * (1.0 - lam_init)).astype(BF16)
        return (base + (qi + 1) * n_diag) % 2

    lax.fori_loop(0, nq, q_block, jnp.int32(0))


def _attn(qk, vt, slopes, lam_params, sub_gain, *, bsz, seq, lam_init, tq):
    t = qk.shape[0]
    tk = vt.shape[3]
    assert tq % tk == 0
    kernel = functools.partial(_attn_kernel, tq=tq, tk=tk, lam_init=lam_init)
    return pl.pallas_call(
        kernel,
        out_shape=jax.ShapeDtypeStruct((t, C_HEADS * C_VALUE_DIM), BF16),
        grid_spec=pltpu.PrefetchScalarGridSpec(
            num_scalar_prefetch=1,
            grid=(bsz, C_HEADS),
            in_specs=[pl.BlockSpec((seq, LANES), lambda b, h, s: (b, h)),
                      pl.BlockSpec((seq, LANES), lambda b, h, s: (b, C_HEADS + h)),
                      pl.BlockSpec((None, seq // tk, V_ROWS, tk), lambda b, h, s: (b, 0, h, 0)),
                      pl.BlockSpec((4, C_HEAD_DIM), lambda b, h, s: (0, 0)),
                      pl.BlockSpec((1, C_VALUE_DIM), lambda b, h, s: (0, 0))],
            out_specs=pl.BlockSpec((seq, LANES), lambda b, h, s: (b, h)),
            scratch_shapes=[pltpu.VMEM((tk, 2 * tq), F32), pltpu.VMEM((tk, 2 * tq), F32),
                            pltpu.VMEM((SUBLANES, 2 * tq), F32), pltpu.VMEM((SUBLANES, 2 * tq), F32),
                            pltpu.VMEM((tk, 2 * tq), BF16),
                            pltpu.VMEM((2 * tq, LANES), BF16),
                            pltpu.VMEM((tk, 2 * tq), F32),
                            pltpu.VMEM((1, 2 * tq), F32),
                            pltpu.VMEM((V_ROWS, 2 * tq), F32)]),
        compiler_params=_params("parallel", "parallel"),
        name="diff_attn",
    )(slopes, qk, qk, vt, lam_params, sub_gain)


def _post_kernel(*refs, n_mix, n_convert, fc):
    h_ref = refs[0]
    mix_refs = refs[1:1 + n_mix]
    (w_out_ref, g_mlp_ref, w1_ref, w2_ref, g_ple_ref, w_gate_ref, w_proj_ref, p_ref) = refs[1 + n_mix:9 + n_mix]
    f32_refs = refs[9 + n_mix:9 + n_mix + n_convert]
    o_ref = refs[9 + n_mix + n_convert]
    for src_ref, dst_ref in zip(f32_refs, refs[10 + n_mix + n_convert:]):
        dst_ref[...] = src_ref[...].astype(BF16)
    mix = mix_refs[0][...] if n_mix == 1 else jnp.concatenate([m[...] for m in mix_refs], axis=1)
    h = h_ref[...] + _dot(mix, w_out_ref[...])
    hn = _rms(h, g_mlp_ref[...]).astype(BF16)
    acc = h
    for c in range(0, D_FF, fc):
        a = jnp.maximum(_dot(hn, w1_ref[:, c:c + fc]), 0.0)
        acc = acc + _dot((a * a).astype(BF16), w2_ref[c:c + fc, :])
    h = acc
    gate = jax.nn.sigmoid(_dot(_rms(h, g_ple_ref[...]).astype(BF16), w_gate_ref[...]))
    o_ref[...] = h + _dot(p_ref[...].astype(BF16), w_proj_ref[...]) * gate


def _post(h, mixes, w_out, g_mlp, w1, w2, g_ple, w_gate, w_proj_all, p_all, layer, convert=(), *,
          tm=512, fc=512):
    t = h.shape[0]
    steps = t // tm
    row = lambda i: (i, 0)
    mix_specs = [pl.BlockSpec((tm, m.shape[1]), row) for m in mixes]
    in_specs = [pl.BlockSpec((tm, D_MODEL), row)] + mix_specs + [
        _resident((D_MODEL, D_MODEL)), _resident((1, D_MODEL)),
        _resident((D_MODEL, D_FF)), _resident((D_FF, D_MODEL)),
        _resident((1, D_MODEL)), _resident((D_MODEL, D_MODEL)),
        _stacked((PLE_DIM, D_MODEL), layer),
        pl.BlockSpec((None, tm, PLE_DIM), lambda i: (layer, i, 0))]
    out_shape = [jax.ShapeDtypeStruct((t, D_MODEL), F32)]
    out_specs = [pl.BlockSpec((tm, D_MODEL), row)]
    args = [h, *mixes, w_out, g_mlp, w1, w2, g_ple, w_gate, w_proj_all, p_all]
    for w_all, index in convert:
        rows, cols = w_all.shape[1] // steps, w_all.shape[2]
        in_specs.append(pl.BlockSpec((None, rows, cols), lambda i, index=index: (index, i, 0)))
        out_shape.append(jax.ShapeDtypeStruct(w_all.shape[1:], BF16))
        out_specs.append(pl.BlockSpec((rows, cols), row))
        args.append(w_all)
    return pl.pallas_call(
        functools.partial(_post_kernel, n_mix=len(mixes), n_convert=len(convert), fc=fc),
        out_shape=out_shape,
        grid=(steps,),
        in_specs=in_specs,
        out_specs=out_specs,
        compiler_params=_params("parallel"),
        name="post",
    )(*args)


def _lane_row(values, offset):
    return jnp.zeros((1, LANES), F32).at[0, offset:offset + values.shape[0]].set(values.astype(F32))


def kernel(x, p, ln_mix_e, w_in_e, gmlp_v_gain, gmlp_ws, gmlp_bs, gdn_conv, gdn_a_log, gdn_dt_bias,
           gdn_out_gain, w_out_e, ln_mix_o, w_qkv_o, attn_q_gain, attn_k_gain, diff_lambda,
           attn_sub_gain, w_out_o, ln_mlp, w_mlp1, w_mlp2, ln_ple, w_ple_gate, w_ple_proj):
    bsz, seq, d = x.shape
    depth = p.shape[0]
    t = bsz * seq
    h = x.reshape(t, d)
    p_all = p.reshape(depth, t, PLE_DIM)
    slopes = 2.0 ** (-8.0 * jnp.arange(1, C_HEADS + 1, dtype=F32) / C_HEADS)
    row = lambda a: a.reshape(1, -1).astype(F32)
    w_in_e16 = jnp.pad(w_in_e, ((0, 0), (0, 0), (0, EVEN_PAD - w_in_e.shape[2]))).astype(BF16)
    w_proj16 = w_ple_proj.astype(BF16)
    w1, w2, w_gate, w_out = (w[0].astype(BF16) for w in (w_mlp1, w_mlp2, w_ple_gate, w_out_e))
    w_qkv = None
    for layer in range(depth):
        i = layer // 2
        if layer % 2 == 0:
            a_out, proj = _in_even(h, row(ln_mix_e[i]), w_in_e16, row(gmlp_v_gain[i]), gmlp_ws,
                                   gmlp_bs[i].T, gdn_conv, i, seq=seq)
            b_out = _gdn(proj, _lane_row(gdn_a_log[i], B_HEADS), _lane_row(gdn_dt_bias[i], B_HEADS),
                         row(gdn_out_gain[i]), bsz=bsz, seq=seq)
            mixes = (a_out, b_out)
        else:
            lam_init = 0.8 - 0.6 * math.exp(-0.3 * layer)
            qg = row(jnp.concatenate([attn_q_gain[i], attn_q_gain[i]]))
            kg = row(jnp.concatenate([attn_k_gain[i], attn_k_gain[i]]))
            qk, vt = _in_odd(h, row(ln_mix_o[i]), w_qkv, qg, kg, bsz=bsz, seq=seq, tk=ATTN_K_BLOCK)
            o = _attn(qk, vt, slopes, diff_lambda[i], row(attn_sub_gain[i]), bsz=bsz, seq=seq,
                      lam_init=lam_init, tq=ATTN_Q_BLOCK)
            mixes = (o,)
        nxt = layer + 1
        convert = []
        if nxt < depth:
            convert = [(w_mlp1, nxt), (w_mlp2, nxt), (w_ple_gate, nxt),
                       (w_out_o if nxt % 2 else w_out_e, nxt // 2)]
            if nxt % 2:
                convert.append((w_qkv_o, nxt // 2))
        h, *converted = _post(h, mixes, w_out, row(ln_mlp[layer]), w1, w2, row(ln_ple[layer]), w_gate,
                              w_proj16, p_all, layer, convert)
        if converted:
            w1, w2, w_gate, w_out = converted[:4]
            w_qkv = converted[4] if nxt % 2 else None
    return h.reshape(bsz, seq, d)
```

```python
import functools
import math

import jax
import jax.numpy as jnp
from jax import lax
from jax.experimental import pallas as pl
from jax.experimental.pallas import tpu as pltpu

F32 = jnp.float32
BF16 = jnp.bfloat16
NORM_EPS = 1e-6

LANES = 128
SUBLANES = 8
MXU_TILE = 256
D_MODEL = 1024
PLE_DIM = 256
D_FF = 4 * D_MODEL
A_WIDTH = 512
A_GROUPS = 4
A_CHUNK = 128
B_HEADS = 4
B_HEAD_DIM = 128
B_WIDTH = 512
B_CONV = 4
B_CHUNK = 64
CONV_PAD = 8
C_HEADS = 8
C_HEAD_DIM = 64
C_VALUE_DIM = 128
ATTN_Q_BLOCK = 512
ATTN_K_BLOCK = 512
ATTN_ROW_CHUNK = 32
V_ROWS = C_VALUE_DIM + 16
LOG2_E = math.log2(math.e)
EVEN_MAIN = 2 * A_WIDTH + 4 * B_WIDTH
EVEN_PAD = EVEN_MAIN + LANES
VMEM_LIMIT = 56 * 1024 * 1024


def _rms(x, gain):
    return x * lax.rsqrt(jnp.mean(x * x, axis=-1, keepdims=True) + NORM_EPS) * gain


def _dot(a, b):
    return jnp.dot(a, b, preferred_element_type=F32)


def _dot_nt(a, b):
    return lax.dot_general(a, b, (((1,), (1,)), ((), ())), preferred_element_type=F32)


def _resident(shape):
    zeros = (0,) * len(shape)
    return pl.BlockSpec(shape, lambda *_: zeros, pipeline_mode=pl.Buffered(1))


def _stacked(shape, index):
    zeros = (0,) * len(shape)
    return pl.BlockSpec((None,) + shape, lambda *_: (index,) + zeros, pipeline_mode=pl.Buffered(1))


def _params(*sem):
    return pltpu.CompilerParams(dimension_semantics=sem, vmem_limit_bytes=VMEM_LIMIT)


def _in_even_kernel(h_ref, g_ref, w_ref, vg_ref, ws_ref, bs_ref, cw_ref, a_ref, o_ref, buf_ref, *,
                    tm, tiles_per_seq):
    @pl.when(pl.program_id(0) % tiles_per_seq == 0)
    def _():
        buf_ref[:, 0:CONV_PAD, :] = jnp.zeros((3, CONV_PAD, B_WIDTH), F32)

    hn = _rms(h_ref[...], g_ref[...]).astype(BF16)
    u = jax.nn.gelu(_dot(hn, w_ref[:, 0:A_WIDTH]))
    v = jax.nn.gelu(_dot(hn, w_ref[:, A_WIDTH:2 * A_WIDTH]))
    for part in range(3):
        cols = slice(part * B_WIDTH, (part + 1) * B_WIDTH)
        x = _dot(hn, w_ref[:, 2 * A_WIDTH + cols.start:2 * A_WIDTH + cols.stop])
        buf_ref[part, CONV_PAD:CONV_PAD + tm, :] = x
        cw = cw_ref[:, cols]
        y = cw[B_CONV - 1:B_CONV, :] * x
        for j in range(B_CONV - 1):
            off = CONV_PAD - (B_CONV - 1) + j
            y = y + cw[j:j + 1, :] * buf_ref[part, off:off + tm, :]
        buf_ref[part, 0:CONV_PAD, :] = x[tm - CONV_PAD:tm, :]
        half = 0.5 * y
        y = half + half * jnp.tanh(half)
        if part == 2:
            o_ref[:, cols] = y
        else:
            for h in range(B_HEADS):
                hcols = slice(cols.start + h * LANES, cols.start + (h + 1) * LANES)
                yh = y[:, h * LANES:(h + 1) * LANES]
                yh = yh * lax.rsqrt(jnp.sum(yh * yh, axis=-1, keepdims=True) + NORM_EPS)
                o_ref[:, hcols] = yh * (B_HEAD_DIM ** -0.5) if part == 0 else yh
    rest = 2 * A_WIDTH + 3 * B_WIDTH
    o_ref[:, 3 * B_WIDTH:] = _dot(hn, w_ref[:, rest:])
    ii = lax.broadcasted_iota(jnp.int32, (A_CHUNK, A_CHUNK), 0)
    jj = lax.broadcasted_iota(jnp.int32, (A_CHUNK, A_CHUNK), 1)
    for g in range(A_GROUPS):
        cols = slice(g * LANES, (g + 1) * LANES)
        w = jnp.where(ii >= jj, ws_ref[g], 0.0).astype(BF16)
        bias = bs_ref[:, g:g + 1]
        gain = vg_ref[:, cols]
        for c in range(tm // A_CHUNK):
            rows = slice(c * A_CHUNK, (c + 1) * A_CHUNK)
            vg = v[rows, cols]
            vc = vg - jnp.mean(vg, axis=-1, keepdims=True)
            y = vc * lax.rsqrt(jnp.mean(vc * vc, axis=-1, keepdims=True) + NORM_EPS) * gain
            s = _dot(w, y.astype(BF16)) + bias
            a_ref[rows, cols] = (u[rows, cols] * s).astype(BF16)


def _in_even(h, gain, w_all, v_gain, ws_all, bs_t, conv_all, i, *, seq, tm=512):
    t = h.shape[0]
    return pl.pallas_call(
        functools.partial(_in_even_kernel, tm=tm, tiles_per_seq=seq // tm),
        out_shape=(jax.ShapeDtypeStruct((t, A_WIDTH), BF16),
                   jax.ShapeDtypeStruct((t, EVEN_PAD - 2 * A_WIDTH), F32)),
        grid=(t // tm,),
        in_specs=[pl.BlockSpec((tm, D_MODEL), lambda i_: (i_, 0)),
                  _resident((1, D_MODEL)), _stacked((D_MODEL, EVEN_PAD), i),
                  _resident((1, A_WIDTH)), _stacked((A_GROUPS, A_CHUNK, A_CHUNK), i),
                  _resident((A_CHUNK, A_GROUPS)), _stacked((B_CONV, 3 * B_WIDTH), i)],
        out_specs=(pl.BlockSpec((tm, A_WIDTH), lambda i_: (i_, 0)),
                   pl.BlockSpec((tm, EVEN_PAD - 2 * A_WIDTH), lambda i_: (i_, 0))),
        scratch_shapes=[pltpu.VMEM((3, CONV_PAD + tm, B_WIDTH), F32)],
        compiler_params=pltpu.CompilerParams(
            dimension_semantics=("arbitrary",), vmem_limit_bytes=VMEM_LIMIT,
            allow_input_fusion=[False, False, True, False, False, False, False]),
        name="in_even",
    )(h, gain, w_all, v_gain, ws_all, bs_t, conv_all)


def _unit_lower_inverses(a_list, eye, level_masks):
    ds = [eye - jnp.where(level_masks[0], a, 0.0) for a in a_list]
    for mask in level_masks[1:]:
        d16 = [d.astype(BF16) for d in ds]
        mds = [_dot(jnp.where(mask, a, 0.0).astype(BF16), d) for a, d in zip(a_list, d16)]
        ds = [d - _dot(d_lo, md.astype(BF16)) for d, d_lo, md in zip(ds, d16, mds)]
    return ds


def _gdn_kernel(qn_ref, kn_ref, vn_ref, z_ref, ba_ref, alog_ref, dtb_ref, og_ref, o_ref, state_ref,
                lhs_ref, ds_ref, o0_ref, gl_ref, raw_ref, *, tb, group):
    c_len = B_CHUNK

    @pl.when(pl.program_id(1) == 0)
    def _():
        state_ref[...] = jnp.zeros_like(state_ref)

    ii = lax.broadcasted_iota(jnp.int32, (c_len, c_len), 0)
    jj = lax.broadcasted_iota(jnp.int32, (c_len, c_len), 1)
    causal = ii >= jj
    strict = ii > jj
    eye = jnp.where(ii == jj, 1.0, 0.0).astype(F32)
    ltri = jnp.where(causal, 1.0, 0.0).astype(BF16)
    level_masks = []
    s = 1
    while s < c_len:
        level_masks.append(((ii // (2 * s)) == (jj // (2 * s))) & ((ii % (2 * s)) >= s) & ((jj % (2 * s)) < s))
        s *= 2
    neg_rate = -jnp.exp(alog_ref[...])
    dt_bias = dtb_ref[...]
    out_gain = og_ref[...]

    heads = range(B_HEADS)

    def prepare(gi, carry):
        prob = []
        for cc in range(group):
            c = gi * group + cc
            rows = pl.ds(pl.multiple_of(c * c_len, c_len), c_len)
            ba = ba_ref[rows, :]
            beta = jax.nn.sigmoid(ba)
            g = neg_rate * jax.nn.softplus(ba + dt_bias)
            g_hi = g.astype(BF16)
            g_lo = (g - g_hi.astype(F32)).astype(BF16)
            gc = _dot(ltri, g_hi) + _dot(ltri, g_lo)
            gc_t = gc.T
            for h in heads:
                cols = slice(h * LANES, (h + 1) * LANES)
                q_h = qn_ref[rows, cols]
                k_h = kn_ref[rows, cols]
                gcol = gc[:, B_HEADS + h:B_HEADS + h + 1]
                grow = gc_t[B_HEADS + h:B_HEADS + h + 1, :]
                glast = gc[c_len - 1:c_len, B_HEADS + h:B_HEADS + h + 1]
                bcol = beta[:, h:h + 1]
                decay = jnp.where(causal, jnp.exp(jnp.where(causal, gcol - grow, 0.0)), 0.0)
                kk = _dot_nt(jnp.concatenate([q_h, k_h * bcol], axis=0).astype(BF16), k_h.astype(BF16))
                prob.append(dict(c=c, h=h, rows=rows, cols=cols, gc=gc, beta=beta,
                                 a=jnp.where(strict, kk[c_len:] * decay, 0.0),
                                 qk=(kk[:c_len] * decay).astype(BF16)))
        t_invs = _unit_lower_inverses([p["a"] for p in prob], eye, level_masks)
        def columns(p):
            lane = B_HEADS + p["h"]
            return (p["gc"][:, lane:lane + 1], p["gc"][c_len - 1:c_len, lane:lane + 1],
                    p["beta"][:, p["h"]:p["h"] + 1])

        wus, k_decs = [], []
        for t_inv, p in zip(t_invs, prob):
            gcol, glast, bcol = columns(p)
            k_h = kn_ref[p["rows"], p["cols"]]
            rhs = jnp.concatenate([k_h * bcol * jnp.exp(gcol), vn_ref[p["rows"], p["cols"]] * bcol], axis=1)
            wus.append(_dot(t_inv.astype(BF16), rhs.astype(BF16)).astype(BF16))
            k_decs.append(k_h * jnp.exp(glast - gcol))
        kd_wu = [_dot(k_dec.T.astype(BF16), wu) for k_dec, wu in zip(k_decs, wus)]
        qk_wu = [_dot(p["qk"], wu) for p, wu in zip(prob, wus)]
        for p, kd, qk in zip(prob, kd_wu, qk_wu):
            c, h = p["c"], p["h"]
            gcol, glast, _ = columns(p)
            q_dec = qn_ref[p["rows"], p["cols"]] * jnp.exp(gcol)
            lhs_ref[c, h] = jnp.concatenate(
                [kd[:, :B_HEAD_DIM], qk[:, :B_HEAD_DIM] - q_dec], axis=0).astype(BF16)
            ds_ref[c, h] = kd[:, B_HEAD_DIM:]
            o0_ref[c, h] = qk[:, B_HEAD_DIM:]
            gl_ref[c, h] = jnp.broadcast_to(jnp.exp(glast), (1, LANES))
        return carry

    lax.fori_loop(0, tb // (c_len * group), prepare, 0)

    def recur(c, carry):
        rows = pl.ds(pl.multiple_of(c * c_len, c_len), c_len)
        states = [state_ref[h] for h in heads]
        prods = [_dot(lhs_ref[c, h], states[h].astype(BF16)) for h in heads]
        for h in heads:
            state_ref[h] = states[h] * gl_ref[c, h] + (ds_ref[c, h] - prods[h][:B_HEAD_DIM])
            raw_ref[rows, h * LANES:(h + 1) * LANES] = o0_ref[c, h] - prods[h][B_HEAD_DIM:]
        return carry

    lax.fori_loop(0, tb // c_len, recur, 0)

    for h in heads:
        cols = slice(h * LANES, (h + 1) * LANES)
        z = z_ref[:, cols]
        o_ref[:, cols] = (_rms(raw_ref[:, cols], out_gain) * (z * jax.nn.sigmoid(z))).astype(BF16)


def _gdn(proj, alog_row, dtb_row, out_gain, *, bsz, seq, tb=512, group=8):
    t = proj.shape[0]
    nt = seq // tb

    def col(j):
        return pl.BlockSpec((tb, B_WIDTH), lambda b, i: (b * nt + i, j))

    nc = tb // B_CHUNK
    per_problem = lambda *shape_dtype: pltpu.VMEM((nc, B_HEADS) + shape_dtype[:-1], shape_dtype[-1])
    return pl.pallas_call(
        functools.partial(_gdn_kernel, tb=tb, group=group),
        out_shape=jax.ShapeDtypeStruct((t, B_WIDTH), BF16),
        grid=(bsz, nt),
        in_specs=[col(0), col(1), col(2), col(3),
                  pl.BlockSpec((tb, LANES), lambda b, i: (b * nt + i, 4 * B_WIDTH // LANES)),
                  pl.BlockSpec((1, LANES), lambda b, i: (0, 0)),
                  pl.BlockSpec((1, LANES), lambda b, i: (0, 0)),
                  pl.BlockSpec((1, B_HEAD_DIM), lambda b, i: (0, 0))],
        out_specs=pl.BlockSpec((tb, B_WIDTH), lambda b, i: (b * nt + i, 0)),
        scratch_shapes=[pltpu.VMEM((B_HEADS, B_HEAD_DIM, B_HEAD_DIM), F32),
                        per_problem(B_HEAD_DIM + B_CHUNK, B_HEAD_DIM, BF16),
                        per_problem(B_HEAD_DIM, B_HEAD_DIM, F32),
                        per_problem(B_CHUNK, B_HEAD_DIM, F32),
                        per_problem(1, LANES, F32),
                        pltpu.VMEM((tb, B_WIDTH), F32)],
        compiler_params=_params("parallel", "arbitrary"),
        name="gdn",
    )(proj, proj, proj, proj, proj, alog_row, dtb_row, out_gain)


def _in_odd_kernel(h_ref, g_ref, w_ref, qg_ref, kg_ref, o_ref, vt_ref, *, tm, tn, tk):
    hn = _rms(h_ref[...], g_ref[...]).astype(BF16)
    lane = lax.broadcasted_iota(jnp.int32, (tm, LANES), 1)
    low = lane < C_HEAD_DIM
    qk_width = 2 * C_HEADS * 2 * C_HEAD_DIM
    for start in range(0, 3 * C_HEADS * C_VALUE_DIM, tn):
        y = _dot(hn, w_ref[:, start:start + tn])
        if start >= qk_width:
            extra = jnp.where(lax.broadcasted_iota(jnp.int32, (V_ROWS - C_VALUE_DIM, tk), 0) == 0, 1.0, 0.0)
            for hh in range(tn // C_VALUE_DIM):
                row0 = ((start - qk_width) // C_VALUE_DIM + hh) * V_ROWS
                for r0 in range(0, tm, MXU_TILE):
                    vt_ref[r0 // tk, row0:row0 + C_VALUE_DIM, r0 % tk:r0 % tk + MXU_TILE] = (
                        y[r0:r0 + MXU_TILE, hh * C_VALUE_DIM:(hh + 1) * C_VALUE_DIM].T.astype(BF16))
                for kb in range(tm // tk):
                    vt_ref[kb, row0 + C_VALUE_DIM:row0 + V_ROWS, :] = extra.astype(BF16)
            continue
        is_q = start < qk_width // 2
        gain = qg_ref[...] if is_q else kg_ref[...]
        for j in range(tn // LANES):
            yb = y[:, j * LANES:(j + 1) * LANES]
            y2 = yb * yb
            lo = jnp.sum(jnp.where(low, y2, 0.0), axis=-1, keepdims=True)
            hi = jnp.sum(jnp.where(low, 0.0, y2), axis=-1, keepdims=True)
            ms = jnp.where(low, lo, hi) * (1.0 / C_HEAD_DIM)
            yn = yb * lax.rsqrt(ms + NORM_EPS) * gain
            if is_q:
                yn = yn * (C_HEAD_DIM ** -0.5 * LOG2_E)
            o_ref[:, start + j * LANES:start + (j + 1) * LANES] = yn.astype(BF16)


def _in_odd(h, gain, w, qg, kg, *, bsz, seq, tk, tm=512, tn=512):
    t = h.shape[0]
    n = w.shape[1]
    nt = seq // tm
    qk_width = 2 * C_HEADS * 2 * C_HEAD_DIM
    return pl.pallas_call(
        functools.partial(_in_odd_kernel, tm=tm, tn=tn, tk=tk),
        out_shape=(jax.ShapeDtypeStruct((t, qk_width), BF16),
                   jax.ShapeDtypeStruct((bsz, seq // tk, C_HEADS * V_ROWS, tk), BF16)),
        grid=(t // tm,),
        in_specs=[pl.BlockSpec((tm, D_MODEL), lambda i: (i, 0)),
                  _resident((1, D_MODEL)), _resident((D_MODEL, n)),
                  _resident((1, LANES)), _resident((1, LANES))],
        out_specs=(pl.BlockSpec((tm, qk_width), lambda i: (i, 0)),
                   pl.BlockSpec((None, tm // tk, C_HEADS * V_ROWS, tk), lambda i: (i // nt, i % nt, 0, 0))),
        compiler_params=_params("parallel"),
        name="in_odd",
    )(h, gain, w, qg, kg)


def _attn_kernel(slopes_ref, q_ref, k_ref, vt_ref, lam_ref, sg_ref, o_ref, s0_ref, s1_ref, mx0_ref,
                 mx1_ref, p_ref, q2_ref, rel_ref, m_ref, acc_ref, *, tq, tk, lam_init):
    slope = slopes_ref[pl.program_id(1)] * LOG2_E
    nq = q_ref.shape[0] // tq
    ki = lax.broadcasted_iota(jnp.int32, (tk, 2 * tq), 0)
    qj = lax.broadcasted_iota(jnp.int32, (tk, 2 * tq), 1)
    qj = jnp.where(qj >= tq, qj - tq, qj)
    rel_ref[...] = (ki - qj).astype(F32) * slope
    lane = lax.broadcasted_iota(jnp.int32, (tq, LANES), 1)
    lp = lam_ref[...]
    lam = (jnp.exp(jnp.sum(lp[0:1] * lp[1:2], axis=-1, keepdims=True))
           - jnp.exp(jnp.sum(lp[2:3] * lp[3:4], axis=-1, keepdims=True)) + lam_init)
    sub_gain = sg_ref[...]

    def load_q2(qi):
        q = q_ref[pl.ds(pl.multiple_of(qi * tq, tq), tq), :]
        zero = jnp.zeros_like(q)
        return jnp.concatenate([jnp.where(lane < C_HEAD_DIM, q, zero),
                                jnp.where(lane < C_HEAD_DIM, zero, q)], axis=0)

    lanes2 = 2 * tq
    n_tile = lanes2 // MXU_TILE
    n_slab = tk // MXU_TILE
    n_diag = tq // tk
    s_bufs = (s0_ref, s1_ref)
    mx_bufs = (mx0_ref, mx1_ref)
    chunk = ATTN_ROW_CHUNK

    def fold_max(x):
        return jnp.max(x.reshape(x.shape[0] // SUBLANES, SUBLANES, x.shape[1]), axis=0)

    def exp_rows(shift, par, rows, cols):
        p_ref[rows, cols] = jnp.exp2((s_bufs[par][rows, cols] + shift[:, cols]).astype(BF16))

    def produce_scores(par, k_blk, q2, j):
        cols = slice(j * MXU_TILE, (j + 1) * MXU_TILE)
        t = _dot_nt(k_blk, q2[cols, :]) + rel_ref[:, cols]
        s_bufs[par][:, cols] = t
        mx_bufs[par][:, cols] = fold_max(t)

    q2_first = load_q2(0)
    q2_ref[...] = q2_first
    for j in range(n_tile):
        produce_scores(0, k_ref[0:tk, :], q2_first, j)

    def q_block(qi, base):
        q0 = pl.multiple_of(qi * tq, tq)
        q2 = q2_ref[...]
        m_ref[...] = jnp.full(m_ref.shape, -jnp.inf, F32)
        acc_ref[...] = jnp.zeros(acc_ref.shape, F32)

        def step(kb, diag, par):
            if diag == n_diag - 1:
                q2_next = load_q2(jnp.minimum(qi + 1, nq - 1))
                q2_ref[...] = q2_next
                k_next = k_ref[0:tk, :]
            else:
                q2_next = q2
                k_next = k_ref[pl.ds(pl.multiple_of((kb + 1) * tk, tk), tk), :]

            def scores_piece(j):
                produce_scores(1 - par, k_next, q2_next, j)

            def lead(j):
                return (j * MXU_TILE) % tq - diag * tk

            def live_slabs(j):
                if diag is None:
                    return n_slab
                return min(max(lead(j) // MXU_TILE + 1, 0), n_slab)

            scores_piece(0)
            if diag is None:
                mx = mx_bufs[par][...]
            else:
                mx_tiles = []
                for j in range(n_tile):
                    cols = slice(j * MXU_TILE, (j + 1) * MXU_TILE)
                    mx_j = jnp.full((SUBLANES, MXU_TILE), -jnp.inf, F32)
                    for r0 in range(0, live_slabs(j) * MXU_TILE, chunk):
                        t = s_bufs[par][r0:r0 + chunk, cols]
                        if r0 // MXU_TILE == lead(j) // MXU_TILE:
                            krow = lax.broadcasted_iota(jnp.int32, (chunk, MXU_TILE), 0) + r0
                            qcol = lax.broadcasted_iota(jnp.int32, (chunk, MXU_TILE), 1) + lead(j)
                            t = jnp.where(krow <= qcol, t, -jnp.inf)
                            s_bufs[par][r0:r0 + chunk, cols] = t
                        mx_j = jnp.maximum(mx_j, fold_max(t))
                    mx_tiles.append(mx_j)
                mx = jnp.concatenate(mx_tiles, axis=1)
            offset = slope * (kb * tk - q0).astype(F32)
            m_prev = m_ref[...]
            m_new = jnp.maximum(m_prev, jnp.max(mx, axis=0, keepdims=True) + offset)
            alpha = jnp.exp2(m_prev - m_new)
            shift = offset - m_new
            pv = [None] * n_tile
            for slab in range(n_slab):
                keys = slice(slab * MXU_TILE, (slab + 1) * MXU_TILE)
                tiles = [j for j in range(n_tile) if slab < live_slabs(j)]
                for r0 in range(keys.start, keys.stop, chunk):
                    for j in tiles:
                        exp_rows(shift, par, slice(r0, r0 + chunk), slice(j * MXU_TILE, (j + 1) * MXU_TILE))
                first = 1 + slab * (n_tile - 1) // n_slab
                for j in range(first, 1 + (slab + 1) * (n_tile - 1) // n_slab):
                    scores_piece(j)
                for j in tiles:
                    part = _dot(vt_ref[kb][:, keys], p_ref[keys, j * MXU_TILE:(j + 1) * MXU_TILE])
                    pv[j] = part if pv[j] is None else pv[j] + part
            dead = jnp.zeros((V_ROWS, MXU_TILE), F32)
            pv = [dead if part is None else part for part in pv]
            acc_ref[...] = alpha * acc_ref[...] + jnp.concatenate(pv, axis=1)
            m_ref[...] = m_new

        def either_parity(kb, diag):
            dyn_par = (base + kb) % 2
            for par in range(2):
                pl.when(dyn_par == par)(functools.partial(step, kb, diag, par))

        def body(kb, c):
            either_parity(kb, None)
            return c

        lax.fori_loop(0, qi * n_diag, body, 0)
        for diag in range(n_diag):
            either_parity(qi * n_diag + diag, diag)
        acc = acc_ref[...]
        o2 = acc[:C_VALUE_DIM] / acc[C_VALUE_DIM:C_VALUE_DIM + 1]
        o_t = o2[:, :tq] - lam * o2[:, tq:]
        o_t = o_t * lax.rsqrt(jnp.mean(o_t * o_t, axis=0, keepdims=True) + NORM_EPS)
        o_ref[pl.ds(q0, tq), :] = (o_t.T * sub_gain * (1.0 - lam_init)).astype(BF16)
        return (base + (qi + 1) * n_diag) % 2

    lax.fori_loop(0, nq, q_block, jnp.int32(0))


def _attn(qk, vt, slopes, lam_params, sub_gain, *, bsz, seq, lam_init, tq):
    t = qk.shape[0]
    tk = vt.shape[3]
    assert tq % tk == 0
    kernel = functools.partial(_attn_kernel, tq=tq, tk=tk, lam_init=lam_init)
    return pl.pallas_call(
        kernel,
        out_shape=jax.ShapeDtypeStruct((t, C_HEADS * C_VALUE_DIM), BF16),
        grid_spec=pltpu.PrefetchScalarGridSpec(
            num_scalar_prefetch=1,
            grid=(bsz, C_HEADS),
            in_specs=[pl.BlockSpec((seq, LANES), lambda b, h, s: (b, h)),
                      pl.BlockSpec((seq, LANES), lambda b, h, s: (b, C_HEADS + h)),
                      pl.BlockSpec((None, seq // tk, V_ROWS, tk), lambda b, h, s: (b, 0, h, 0)),
                      pl.BlockSpec((4, C_HEAD_DIM), lambda b, h, s: (0, 0)),
                      pl.BlockSpec((1, C_VALUE_DIM), lambda b, h, s: (0, 0))],
            out_specs=pl.BlockSpec((seq, LANES), lambda b, h, s: (b, h)),
            scratch_shapes=[pltpu.VMEM((tk, 2 * tq), F32), pltpu.VMEM((tk, 2 * tq), F32),
                            pltpu.VMEM((SUBLANES, 2 * tq), F32), pltpu.VMEM((SUBLANES, 2 * tq), F32),
                            pltpu.VMEM((tk, 2 * tq), BF16),
                            pltpu.VMEM((2 * tq, LANES), BF16),
                            pltpu.VMEM((tk, 2 * tq), F32),
                            pltpu.VMEM((1, 2 * tq), F32),
                            pltpu.VMEM((V_ROWS, 2 * tq), F32)]),
        compiler_params=_params("parallel", "parallel"),
        name="diff_attn",
    )(slopes, qk, qk, vt, lam_params, sub_gain)


def _post_kernel(*refs, n_mix, n_convert, fc):
    h_ref = refs[0]
    mix_refs = refs[1:1 + n_mix]
    (w_out_ref, g_mlp_ref, w1_ref, w2_ref, g_ple_ref, w_gate_ref, w_proj_ref, p_ref) = refs[1 + n_mix:9 + n_mix]
    f32_refs = refs[9 + n_mix:9 + n_mix + n_convert]
    o_ref = refs[9 + n_mix + n_convert]
    for src_ref, dst_ref in zip(f32_refs, refs[10 + n_mix + n_convert:]):
        dst_ref[...] = src_ref[...].astype(BF16)
    mix = mix_refs[0][...] if n_mix == 1 else jnp.concatenate([m[...] for m in mix_refs], axis=1)
    h = h_ref[...] + _dot(mix, w_out_ref[...])
    hn = _rms(h, g_mlp_ref[...]).astype(BF16)
    acc = h
    for c in range(0, D_FF, fc):
        a = jnp.maximum(_dot(hn, w1_ref[:, c:c + fc]), 0.0)
        acc = acc + _dot((a * a).astype(BF16), w2_ref[c:c + fc, :])
    h = acc
    gate = jax.nn.sigmoid(_dot(_rms(h, g_ple_ref[...]).astype(BF16), w_gate_ref[...]))
    o_ref[...] = h + _dot(p_ref[...].astype(BF16), w_proj_ref[...]) * gate


def _post(h, mixes, w_out, g_mlp, w1, w2, g_ple, w_gate, w_proj_all, p_all, layer, convert=(), *,
          tm=512, fc=512):
    t = h.shape[0]
    steps = t // tm
    row = lambda i: (i, 0)
    mix_specs = [pl.BlockSpec((tm, m.shape[1]), row) for m in mixes]
    in_specs = [pl.BlockSpec((tm, D_MODEL), row)] + mix_specs + [
        _resident((D_MODEL, D_MODEL)), _resident((1, D_MODEL)),
        _resident((D_MODEL, D_FF)), _resident((D_FF, D_MODEL)),
        _resident((1, D_MODEL)), _resident((D_MODEL, D_MODEL)),
        _stacked((PLE_DIM, D_MODEL), layer),
        pl.BlockSpec((None, tm, PLE_DIM), lambda i: (layer, i, 0))]
    out_shape = [jax.ShapeDtypeStruct((t, D_MODEL), F32)]
    out_specs = [pl.BlockSpec((tm, D_MODEL), row)]
    args = [h, *mixes, w_out, g_mlp, w1, w2, g_ple, w_gate, w_proj_all, p_all]
    for w_all, index in convert:
        rows, cols = w_all.shape[1] // steps, w_all.shape[2]
        in_specs.append(pl.BlockSpec((None, rows, cols), lambda i, index=index: (index, i, 0)))
        out_shape.append(jax.ShapeDtypeStruct(w_all.shape[1:], BF16))
        out_specs.append(pl.BlockSpec((rows, cols), row))
        args.append(w_all)
    return pl.pallas_call(
        functools.partial(_post_kernel, n_mix=len(mixes), n_convert=len(convert), fc=fc),
        out_shape=out_shape,
        grid=(steps,),
        in_specs=in_specs,
        out_specs=out_specs,
        compiler_params=_params("parallel"),
        name="post",
    )(*args)


def _lane_row(values, offset):
    return jnp.zeros((1, LANES), F32).at[0, offset:offset + values.shape[0]].set(values.astype(F32))


def kernel(x, p, ln_mix_e, w_in_e, gmlp_v_gain, gmlp_ws, gmlp_bs, gdn_conv, gdn_a_log, gdn_dt_bias,
           gdn_out_gain, w_out_e, ln_mix_o, w_qkv_o, attn_q_gain, attn_k_gain, diff_lambda,
           attn_sub_gain, w_out_o, ln_mlp, w_mlp1, w_mlp2, ln_ple, w_ple_gate, w_ple_proj):
    bsz, seq, d = x.shape
    depth = p.shape[0]
    t = bsz * seq
    h = x.reshape(t, d)
    p_all = p.reshape(depth, t, PLE_DIM)
    slopes = 2.0 ** (-8.0 * jnp.arange(1, C_HEADS + 1, dtype=F32) / C_HEADS)
    row = lambda a: a.reshape(1, -1).astype(F32)
    w_in_e16 = jnp.pad(w_in_e, ((0, 0), (0, 0), (0, EVEN_PAD - w_in_e.shape[2]))).astype(BF16)
    w_proj16 = w_ple_proj.astype(BF16)
    w1, w2, w_gate, w_out = (w[0].astype(BF16) for w in (w_mlp1, w_mlp2, w_ple_gate, w_out_e))
    w_qkv = None
    for layer in range(depth):
        i = layer // 2
        if layer % 2 == 0:
            a_out, proj = _in_even(h, row(ln_mix_e[i]), w_in_e16, row(gmlp_v_gain[i]), gmlp_ws,
                                   gmlp_bs[i].T, gdn_conv, i, seq=seq)
            b_out = _gdn(proj, _lane_row(gdn_a_log[i], B_HEADS), _lane_row(gdn_dt_bias[i], B_HEADS),
                         row(gdn_out_gain[i]), bsz=bsz, seq=seq)
            mixes = (a_out, b_out)
        else:
            lam_init = 0.8 - 0.6 * math.exp(-0.3 * layer)
            qg = row(jnp.concatenate([attn_q_gain[i], attn_q_gain[i]]))
            kg = row(jnp.concatenate([attn_k_gain[i], attn_k_gain[i]]))
            qk, vt = _in_odd(h, row(ln_mix_o[i]), w_qkv, qg, kg, bsz=bsz, seq=seq, tk=ATTN_K_BLOCK)
            o = _attn(qk, vt, slopes, diff_lambda[i], row(attn_sub_gain[i]), bsz=bsz, seq=seq,
                      lam_init=lam_init, tq=ATTN_Q_BLOCK)
            mixes = (o,)
        nxt = layer + 1
        convert = []
        if nxt < depth:
            convert = [(w_mlp1, nxt), (w_mlp2, nxt), (w_ple_gate, nxt),
                       (w_out_o if nxt % 2 else w_out_e, nxt // 2)]
            if nxt % 2:
                convert.append((w_qkv_o, nxt // 2))
        h, *converted = _post(h, mixes, w_out, row(ln_mlp[layer]), w1, w2, row(ln_ple[layer]), w_gate,
                              w_proj16, p_all, layer, convert)
        if converted:
            w1, w2, w_gate, w_out = converted[:4]
            w_qkv = converted[4] if nxt % 2 else None
    return h.reshape(bsz, seq, d)
```
